```python
import math
import jax, jax.numpy as jnp
from jax import lax
import numpy as np

D_MODEL = 2048
BATCH = 4
SEQ = 2048
DEPTH = 2
DEC_BATCH = 128
DEC_SEQ = 1
PAST_LEN = 8192
PAGE_SIZE = 128

N_MIXERS = 2
N_HEADS = 16
Q_LORA = 512
KV_LORA = 512
QK_NOPE = 128
QK_ROPE = 64
V_DIM = 128
ATTN_SCALE = 1.0 / math.sqrt(QK_NOPE + QK_ROPE)
ROPE_BASE = 10000.0
CONV_W = 31
D_FF = 5632
FFN_CONV_W = 3
Q_BLOCK = 128
EPS = 1e-6
N_MLA = (DEPTH + 1) // 2
N_CONV = DEPTH // 2

kernel_name = "mla_conformer_convffn_adaln_step"


def rmsnorm(x, g):
    xf = x.astype(jnp.float32)
    y = xf * lax.rsqrt(jnp.mean(xf * xf, axis=-1, keepdims=True) + EPS)
    return (y * g.astype(jnp.float32)).astype(x.dtype)


def layernorm(x, g, b):
    xf = x.astype(jnp.float32)
    mu = jnp.mean(xf, axis=-1, keepdims=True)
    var = jnp.mean(jnp.square(xf - mu), axis=-1, keepdims=True)
    y = (xf - mu) * lax.rsqrt(var + EPS) * g.astype(jnp.float32) + b.astype(jnp.float32)
    return y.astype(x.dtype)


def rope(x, pos):
    half = QK_ROPE // 2
    inv = ROPE_BASE ** (-jnp.arange(half, dtype=jnp.float32) / half)
    ang = pos.astype(jnp.float32)[:, None] * inv[None, :]
    shape = (1, ang.shape[0]) + (1,) * (x.ndim - 3) + (half,)
    cos, sin = jnp.cos(ang).reshape(shape), jnp.sin(ang).reshape(shape)
    xf = x.astype(jnp.float32)
    x1, x2 = xf[..., :half], xf[..., half:]
    return jnp.concatenate([x1 * cos - x2 * sin, x1 * sin + x2 * cos], axis=-1).astype(x.dtype)


def causal_dwconv(u_ext, w, b):
    C = u_ext.shape[-1]
    y = lax.conv_general_dilated(u_ext, w[:, None, :].astype(u_ext.dtype), window_strides=(1,),
                                 padding='VALID', dimension_numbers=('NWC', 'WIO', 'NWC'),
                                 feature_group_count=C)
    return y + b


def modulation(c, w_mod, b_mod):
    m = jax.nn.silu(c) @ w_mod + b_mod
    return jnp.split(m[:, None, :], 6, axis=-1)


def mla_project(h, pos, wq_a, q_norm_g, wq_b, wkv_a, kv_norm_g):
    B, T, _ = h.shape
    q = (rmsnorm(h @ wq_a, q_norm_g) @ wq_b).reshape(B, T, N_HEADS, QK_NOPE + QK_ROPE)
    q_nope, q_pe = q[..., :QK_NOPE], rope(q[..., QK_NOPE:], pos)
    kv = h @ wkv_a
    ckv = rmsnorm(kv[..., :KV_LORA], kv_norm_g)
    kpe = rope(kv[..., KV_LORA:], pos)
    return q_nope, q_pe, ckv, kpe


def mla_prompt(q_nope, q_pe, ckv, kpe, w_uk, w_uv, w_o):
    B, S = ckv.shape[0], ckv.shape[1]
    k_nope = jnp.einsum('bsc,chd->bshd', ckv, w_uk)
    v = jnp.einsum('bsc,chd->bshd', ckv, w_uv)
    nb = S // Q_BLOCK
    qn_b = q_nope.reshape(B, nb, Q_BLOCK, N_HEADS, QK_NOPE).transpose(1, 0, 2, 3, 4)
    qp_b = q_pe.reshape(B, nb, Q_BLOCK, N_HEADS, QK_ROPE).transpose(1, 0, 2, 3, 4)
    k_pos = jnp.arange(S)

    def block(args):
        qn, qp, i = args
        s = (jnp.einsum('bqhd,bkhd->bhqk', qn, k_nope)
             + jnp.einsum('bqhr,bkr->bhqk', qp, kpe)).astype(jnp.float32) * ATTN_SCALE
        q_pos = i * Q_BLOCK + jnp.arange(Q_BLOCK)
        s = jnp.where(k_pos[None, :] <= q_pos[:, None], s, -jnp.inf)
        p = jax.nn.softmax(s, axis=-1).astype(v.dtype)
        return jnp.einsum('bhqk,bkhd->bqhd', p, v)

    o = lax.map(block, (qn_b, qp_b, jnp.arange(nb)))
    o = o.transpose(1, 0, 2, 3, 4).reshape(B, S, N_HEADS * V_DIM)
    return o @ w_o


def mla_sample(q_nope, q_pe, ckv_new, kpe_new, cache_ckv, cache_kpe, j, page_table, w_uk, w_uv, w_o):
    Bd, T = ckv_new.shape[0], ckv_new.shape[1]
    ckv_past = cache_ckv[j, page_table].reshape(Bd, -1, KV_LORA)
    kpe_past = cache_kpe[j, page_table].reshape(Bd, -1, QK_ROPE)
    P = ckv_past.shape[1]
    q_lat = jnp.einsum('bthd,chd->bthc', q_nope, w_uk)
    s_past = (jnp.einsum('bthc,bsc->bhts', q_lat, ckv_past)
              + jnp.einsum('bthr,bsr->bhts', q_pe, kpe_past)).astype(jnp.float32)
    s_new = (jnp.einsum('bthc,bsc->bhts', q_lat, ckv_new)
             + jnp.einsum('bthr,bsr->bhts', q_pe, kpe_new)).astype(jnp.float32)
    causal = jnp.arange(T)[None, :] <= jnp.arange(T)[:, None]
    s_new = jnp.where(causal, s_new, -jnp.inf)
    s = jnp.concatenate([s_past, s_new], axis=-1) * ATTN_SCALE
    p = jax.nn.softmax(s, axis=-1).astype(ckv_new.dtype)
    o_lat = (jnp.einsum('bhts,bsc->bthc', p[..., :P], ckv_past)
             + jnp.einsum('bhts,bsc->bthc', p[..., P:], ckv_new))
    o = jnp.einsum('bthc,chd->bthd', o_lat, w_uv).reshape(Bd, T, N_HEADS * V_DIM)
    return o @ w_o


def conv_module(h, state, w_pw1, b_pw1, w_dw, b_dw, ln_g, ln_b, w_pw2, b_pw2):
    u = h @ w_pw1 + b_pw1
    u = u[..., :D_MODEL] * jax.nn.sigmoid(u[..., D_MODEL:])
    u_ext = jnp.concatenate([state, u], axis=1)
    y = jax.nn.silu(layernorm(causal_dwconv(u_ext, w_dw, b_dw), ln_g, ln_b))
    return y @ w_pw2 + b_pw2, u_ext[:, -(CONV_W - 1):]


def conv_ffn(h, state, w_gate, w_up, w_conv, b_conv, w_down):
    g = h @ w_gate
    u = h @ w_up
    g_ext = jnp.concatenate([state, g], axis=1)
    g = causal_dwconv(g_ext, w_conv, b_conv)
    return (jax.nn.silu(g) * u) @ w_down, g_ext[:, -(FFN_CONV_W - 1):]


def trunk(x, c, W, cache_ckv=None, cache_kpe=None, page_table=None, state_conv=None, state_ffn=None):
    sample = page_table is not None
    B, T, _ = x.shape
    if sample:
        past_len = page_table.shape[1] * cache_ckv.shape[2]
        pos = past_len + jnp.arange(T, dtype=jnp.int32)
    else:
        pos = jnp.arange(T, dtype=jnp.int32)
    new_ckv, new_kpe, new_conv, new_ffn = [], [], [], []
    for i in range(DEPTH):
        sh_a, sc_a, g_a, sh_f, sc_f, g_f = modulation(c, W['w_mod'][i], W['b_mod'][i])
        h = rmsnorm(x, W['norm_mix_g'][i]) * (1 + sc_a) + sh_a
        j = i // N_MIXERS
        if i % N_MIXERS == 0:
            q_nope, q_pe, ckv, kpe = mla_project(h, pos, W['wq_a'][j], W['q_norm_g'][j], W['wq_b'][j],
                                                 W['wkv_a'][j], W['kv_norm_g'][j])
            if sample:
                out = mla_sample(q_nope, q_pe, ckv, kpe, cache_ckv, cache_kpe, j, page_table,
                                 W['w_uk'][j], W['w_uv'][j], W['w_o'][j])
            else:
                out = mla_prompt(q_nope, q_pe, ckv, kpe, W['w_uk'][j], W['w_uv'][j], W['w_o'][j])
            new_ckv.append(ckv)
            new_kpe.append(kpe)
        else:
            st = state_conv[j] if sample else jnp.zeros((B, CONV_W - 1, D_MODEL), x.dtype)
            out, st_new = conv_module(h, st, W['conv_w_pw1'][j], W['conv_b_pw1'][j], W['conv_w_dw'][j],
                                      W['conv_b_dw'][j], W['conv_ln_g'][j], W['conv_ln_b'][j],
                                      W['conv_w_pw2'][j], W['conv_b_pw2'][j])
            new_conv.append(st_new)
        x = x + g_a * out
        h = rmsnorm(x, W['norm_ffn_g'][i]) * (1 + sc_f) + sh_f
        st = state_ffn[i] if sample else jnp.zeros((B, FFN_CONV_W - 1, D_FF), x.dtype)
        out, st_new = conv_ffn(h, st, W['ffn_w_gate'][i], W['ffn_w_up'][i], W['ffn_w_conv'][i],
                               W['ffn_b_conv'][i], W['ffn_w_down'][i])
        new_ffn.append(st_new)
        x = x + g_f * out
    y = rmsnorm(x, W['final_norm_g'])
    return y, jnp.stack(new_ckv), jnp.stack(new_kpe), jnp.stack(new_conv), jnp.stack(new_ffn)


def setup_inputs(seed: int = 0) -> dict:
    key = jax.random.key(seed)
    keys = jax.random.split(key, 48)
    counter = [0]

    def nk():
        k = keys[counter[0]]
        counter[0] += 1
        return k

    def nrm(shape, scale):
        return jax.random.normal(nk(), shape, jnp.float32) * scale

    def gain(shape):
        return 1.0 + nrm(shape, 0.05)

    n_pages = PAST_LEN // PAGE_SIZE
    n_used = DEC_BATCH * n_pages
    n_pool = n_used + max(1, n_used // 4)
    page_table = jax.random.permutation(nk(), n_pool)[:n_used].reshape(DEC_BATCH, n_pages).astype(jnp.int32)
    D = D_MODEL
    HV = N_HEADS * V_DIM
    return {
        'x_prompt': nrm((BATCH, SEQ, D), 1.0),
        'x_sample': nrm((DEC_BATCH, DEC_SEQ, D), 1.0),
        'cache_ckv': nrm((N_MLA, n_pool, PAGE_SIZE, KV_LORA), 1.0),
        'cache_kpe': nrm((N_MLA, n_pool, PAGE_SIZE, QK_ROPE), 1.0),
        'state_conv': nrm((N_CONV, DEC_BATCH, CONV_W - 1, D), 0.5),
        'state_ffn': nrm((DEPTH, DEC_BATCH, FFN_CONV_W - 1, D_FF), 0.5),
        'page_table': page_table,
        'c_prompt': nrm((BATCH, D), 1.0),
        'c_sample': nrm((DEC_BATCH, D), 1.0),
        'w_mod': nrm((DEPTH, D, 6 * D), 0.5 * D ** -0.5),
        'b_mod': nrm((DEPTH, 6 * D), 0.01),
        'norm_mix_g': gain((DEPTH, D)),
        'norm_ffn_g': gain((DEPTH, D)),
        'wq_a': nrm((N_MLA, D, Q_LORA), D ** -0.5),
        'q_norm_g': gain((N_MLA, Q_LORA)),
        'wq_b': nrm((N_MLA, Q_LORA, N_HEADS * (QK_NOPE + QK_ROPE)), Q_LORA ** -0.5),
        'wkv_a': nrm((N_MLA, D, KV_LORA + QK_ROPE), D ** -0.5),
        'kv_norm_g': gain((N_MLA, KV_LORA)),
        'w_uk': nrm((N_MLA, KV_LORA, N_HEADS, QK_NOPE), KV_LORA ** -0.5),
        'w_uv': nrm((N_MLA, KV_LORA, N_HEADS, V_DIM), KV_LORA ** -0.5),
        'w_o': nrm((N_MLA, HV, D), HV ** -0.5),
        'conv_w_pw1': nrm((N_CONV, D, 2 * D), D ** -0.5),
        'conv_b_pw1': nrm((N_CONV, 2 * D), 0.01),
        'conv_w_dw': nrm((N_CONV, CONV_W, D), CONV_W ** -0.5),
        'conv_b_dw': nrm((N_CONV, D), 0.01),
        'conv_ln_g': gain((N_CONV, D)),
        'conv_ln_b': nrm((N_CONV, D), 0.01),
        'conv_w_pw2': nrm((N_CONV, D, D), D ** -0.5),
        'conv_b_pw2': nrm((N_CONV, D), 0.01),
        'ffn_w_gate': nrm((DEPTH, D, D_FF), D ** -0.5),
        'ffn_w_up': nrm((DEPTH, D, D_FF), D ** -0.5),
        'ffn_w_conv': nrm((DEPTH, FFN_CONV_W, D_FF), FFN_CONV_W ** -0.5),
        'ffn_b_conv': nrm((DEPTH, D_FF), 0.01),
        'ffn_w_down': nrm((DEPTH, D_FF, D), D_FF ** -0.5),
        'final_norm_g': gain((D,)),
    }


def reference(x_prompt, x_sample, cache_ckv, cache_kpe, state_conv, state_ffn, page_table, c_prompt, c_sample,
              w_mod, b_mod, norm_mix_g, norm_ffn_g, wq_a, q_norm_g, wq_b, wkv_a, kv_norm_g, w_uk, w_uv, w_o,
              conv_w_pw1, conv_b_pw1, conv_w_dw, conv_b_dw, conv_ln_g, conv_ln_b, conv_w_pw2, conv_b_pw2,
              ffn_w_gate, ffn_w_up, ffn_w_conv, ffn_b_conv, ffn_w_down, final_norm_g):
    W = dict(w_mod=w_mod, b_mod=b_mod, norm_mix_g=norm_mix_g, norm_ffn_g=norm_ffn_g, wq_a=wq_a,
             q_norm_g=q_norm_g, wq_b=wq_b, wkv_a=wkv_a, kv_norm_g=kv_norm_g, w_uk=w_uk, w_uv=w_uv, w_o=w_o,
             conv_w_pw1=conv_w_pw1, conv_b_pw1=conv_b_pw1, conv_w_dw=conv_w_dw, conv_b_dw=conv_b_dw,
             conv_ln_g=conv_ln_g, conv_ln_b=conv_ln_b, conv_w_pw2=conv_w_pw2, conv_b_pw2=conv_b_pw2,
             ffn_w_gate=ffn_w_gate, ffn_w_up=ffn_w_up, ffn_w_conv=ffn_w_conv, ffn_b_conv=ffn_b_conv,
             ffn_w_down=ffn_w_down, final_norm_g=final_norm_g)
    y_prompt, p_ckv, p_kpe, p_conv, p_ffn = trunk(x_prompt, c_prompt, W)
    y_sample, s_ckv, s_kpe, s_conv, s_ffn = trunk(x_sample, c_sample, W, cache_ckv=cache_ckv,
                                                  cache_kpe=cache_kpe, page_table=page_table,
                                                  state_conv=state_conv, state_ffn=state_ffn)
    return (y_prompt, y_sample, p_ckv, p_kpe, s_ckv, s_kpe, p_conv, s_conv, p_ffn, s_ffn)
```

```python
import functools
import math
from typing import NamedTuple

import jax
import jax.numpy as jnp
from jax import lax
from jax.experimental import pallas as pl
from jax.experimental.pallas import tpu as pltpu

D_MODEL = 2048
N_HEADS = 16
Q_LORA = 512
KV_LORA = 512
QK_NOPE = 128
QK_ROPE = 64
V_DIM = 128
ATTN_SCALE = 1.0 / math.sqrt(QK_NOPE + QK_ROPE)
ROPE_BASE = 10000.0
CONV_W = 31
D_FF = 5632
EPS = 1e-6

HEAD_W = 256
LANE = 128
HALO_FFN = 8
HALO_CONV = 32
V7X_VMEM_LIMIT = 56 * 1024 * 1024

BF = jnp.bfloat16
F32 = jnp.float32


class Grp(NamedTuple):
    G: int
    S: int
    per_row: bool


def _params(*sem):
    return pltpu.CompilerParams(dimension_semantics=sem, vmem_limit_bytes=V7X_VMEM_LIMIT)


def _dot(a, b):
    return jnp.dot(a, b, preferred_element_type=F32)


def _dot_nt(a, b):
    return lax.dot_general(a, b, (((1,), (1,)), ((), ())), preferred_element_type=F32)


def _rms(x, g):
    return x * lax.rsqrt(jnp.mean(x * x, axis=-1, keepdims=True) + EPS) * g


def _silu(x):
    return x * jax.nn.sigmoid(x)


def _modulated_norm(x, g, sc, sh):
    return (_rms(x, g) * (1.0 + sc) + sh).astype(BF)


def _mod_spec(grp, ts, k):
    if grp.per_row:
        return pl.BlockSpec((None, ts, D_MODEL), lambda g, t, *_: (g, t, k))
    return pl.BlockSpec((None, 1, D_MODEL), lambda g, t, *_: (g, 0, k))


def _row_spec(ts, width):
    return pl.BlockSpec((None, ts, width), lambda g, t, *_: (g, t, 0))


def _const_spec(shape):
    nd = len(shape)
    return pl.BlockSpec(shape, lambda *_: (0,) * nd)


def _mod_kernel(c_ref, w_ref, b_ref, o_ref):
    c = c_ref[...]
    o_ref[...] = _dot(_silu(c).astype(BF), w_ref[...].astype(BF)) + b_ref[...]


def _modulation(c_all, w_mod, b_mod):
    L, D, N = w_mod.shape
    R = c_all.shape[0]
    tn = 1024
    return pl.pallas_call(
        _mod_kernel,
        out_shape=jax.ShapeDtypeStruct((L, R, N), F32),
        grid=(L, N // tn),
        in_specs=[pl.BlockSpec((R, D), lambda l, j: (0, 0)),
                  pl.BlockSpec((None, D, tn), lambda l, j: (l, 0, j)),
                  pl.BlockSpec((None, 1, tn), lambda l, j: (l, 0, j))],
        out_specs=pl.BlockSpec((None, R, tn), lambda l, j: (l, 0, j)),
        compiler_params=_params("arbitrary", "arbitrary"),
        name="modulation",
    )(c_all, w_mod, b_mod.reshape(L, 1, N))


def _proj_kernel(*refs, expand_kv):
    if expand_kv:
        (x_ref, sh_ref, sc_ref, cos_ref, sin_ref, ng_ref, wa_ref, qg_ref, wqb_ref, kvg_ref,
         wuk_ref, wuv_ref, q_ref, ckv_ref, kpe_ref, k_ref, v_ref) = refs
    else:
        (x_ref, sh_ref, sc_ref, cos_ref, sin_ref, ng_ref, wa_ref, qg_ref, wqb_ref, kvg_ref,
         q_ref, ckv_ref, kpe_ref) = refs
    h = _modulated_norm(x_ref[...], ng_ref[...], sc_ref[...], sh_ref[...])
    qkv = _dot(h, wa_ref[...])
    qn = _rms(qkv[:, :Q_LORA], qg_ref[...]).astype(BF)
    ckv = _rms(qkv[:, Q_LORA:Q_LORA + KV_LORA], kvg_ref[...])
    cos = cos_ref[...]
    sin = sin_ref[...]
    o = Q_LORA + KV_LORA
    kpe = qkv[:, o:o + LANE] * cos + qkv[:, o + LANE:o + 2 * LANE] * sin
    ckv_ref[...] = ckv
    kpe_ref[...] = kpe[:, :QK_ROPE]
    qm = _dot(qn, wqb_ref[...])
    sw0 = N_HEADS * HEAD_W
    for hd in range(N_HEADS):
        a = hd * HEAD_W
        q_ref[:, a:a + LANE] = (qm[:, a:a + LANE] * ATTN_SCALE).astype(BF)
        pe = qm[:, a + LANE:a + 2 * LANE] * cos + qm[:, sw0 + hd * LANE:sw0 + (hd + 1) * LANE] * sin
        q_ref[:, a + LANE:a + 2 * LANE] = (pe * ATTN_SCALE).astype(BF)
    if expand_kv:
        cb = ckv.astype(BF)
        kn = _dot(cb, wuk_ref[...])
        kpe_b = kpe.astype(BF)
        for hd in range(N_HEADS):
            a = hd * HEAD_W
            k_ref[:, a:a + LANE] = kn[:, hd * LANE:(hd + 1) * LANE].astype(BF)
            k_ref[:, a + LANE:a + 2 * LANE] = kpe_b
        v_ref[...] = _dot(cb, wuv_ref[...]).astype(BF)


def _mla_project(grp, ts, x, mod, cos, sin, norm_g, w_a, q_norm_g, wqb, kv_norm_g, w_uk=None, w_uv=None):
    expand_kv = w_uk is not None
    G, S = grp.G, grp.S
    qw = N_HEADS * HEAD_W
    in_specs = [_row_spec(ts, D_MODEL), _mod_spec(grp, ts, 0), _mod_spec(grp, ts, 1),
                pl.BlockSpec((ts, LANE), lambda g, t: (t, 0)), pl.BlockSpec((ts, LANE), lambda g, t: (t, 0)),
                _const_spec((1, D_MODEL)), _const_spec(w_a.shape), _const_spec((1, Q_LORA)),
                _const_spec(wqb.shape), _const_spec((1, KV_LORA))]
    args = [x, mod, mod, cos, sin, norm_g, w_a, q_norm_g, wqb, kv_norm_g]
    out_shape = [jax.ShapeDtypeStruct((G, S, qw), BF), jax.ShapeDtypeStruct((G, S, KV_LORA), F32),
                 jax.ShapeDtypeStruct((G, S, QK_ROPE), F32)]
    out_specs = [_row_spec(ts, qw), _row_spec(ts, KV_LORA), _row_spec(ts, QK_ROPE)]
    if expand_kv:
        in_specs += [_const_spec(w_uk.shape), _const_spec(w_uv.shape)]
        args += [w_uk, w_uv]
        out_shape += [jax.ShapeDtypeStruct((G, S, qw), BF), jax.ShapeDtypeStruct((G, S, N_HEADS * V_DIM), BF)]
        out_specs += [_row_spec(ts, qw), _row_spec(ts, N_HEADS * V_DIM)]
    return pl.pallas_call(
        functools.partial(_proj_kernel, expand_kv=expand_kv),
        out_shape=out_shape, grid=(G, S // ts), in_specs=in_specs, out_specs=out_specs,
        compiler_params=_params("arbitrary", "arbitrary"),
        name="mla_project_kv" if expand_kv else "mla_project",
    )(*args)


def _attn_kernel(q_ref, k_ref, v_ref, o_ref, m_sc, l_sc, acc_sc, *, tq):
    qi = pl.program_id(2)
    q = q_ref[...]
    m_sc[...] = jnp.full(m_sc.shape, -jnp.inf, F32)
    l_sc[...] = jnp.zeros(l_sc.shape, F32)
    acc_sc[...] = jnp.zeros(acc_sc.shape, F32)

    def step(j, masked):
        off = pl.multiple_of(j * tq, tq)
        s = _dot_nt(q, k_ref[pl.ds(off, tq), :])
        if masked:
            row = lax.broadcasted_iota(jnp.int32, (tq, tq), 0)
            col = lax.broadcasted_iota(jnp.int32, (tq, tq), 1)
            s = jnp.where(col <= row, s, -jnp.inf)
        m_prev = m_sc[...]
        m_new = jnp.maximum(m_prev, jnp.max(s, axis=-1, keepdims=True))
        alpha = jnp.exp(m_prev - m_new)
        p = jnp.exp(s - m_new)
        l_sc[...] = alpha * l_sc[...] + jnp.sum(p, axis=-1, keepdims=True)
        acc_sc[...] = alpha * acc_sc[...] + _dot(p.astype(BF), v_ref[pl.ds(off, tq), :])
        m_sc[...] = m_new

    def body(j, carry):
        step(j, False)
        return carry

    lax.fori_loop(0, qi, body, 0)
    step(qi, True)
    o_ref[...] = (acc_sc[...] / l_sc[...]).astype(o_ref.dtype)


def _prompt_attention(q, k, v, tq):
    B, S, _ = q.shape
    return pl.pallas_call(
        functools.partial(_attn_kernel, tq=tq),
        out_shape=jax.ShapeDtypeStruct((B, S, N_HEADS * V_DIM), BF),
        grid=(B, N_HEADS, S // tq),
        in_specs=[pl.BlockSpec((None, tq, HEAD_W), lambda b, h, i: (b, i, h)),
                  pl.BlockSpec((None, S, HEAD_W), lambda b, h, i: (b, 0, h)),
                  pl.BlockSpec((None, S, V_DIM), lambda b, h, i: (b, 0, h))],
        out_specs=pl.BlockSpec((None, tq, V_DIM), lambda b, h, i: (b, i, h)),
        scratch_shapes=[pltpu.VMEM((tq, 1), F32), pltpu.VMEM((tq, 1), F32), pltpu.VMEM((tq, V_DIM), F32)],
        compiler_params=_params("arbitrary", "arbitrary", "arbitrary"),
        name="prompt_attention",
    )(q, k, v)


def _absorb_kernel(q_ref, wukt_ref, qlat_ref, qpe_ref):
    q = q_ref[...]
    qlat_ref[...] = _dot(q[:, :QK_NOPE], wukt_ref[...]).astype(BF)
    qpe_ref[...] = q[:, QK_NOPE:]


def _absorb_q(q, w_ukt):
    Bd = q.shape[0]
    return pl.pallas_call(
        _absorb_kernel,
        out_shape=[jax.ShapeDtypeStruct((Bd, N_HEADS * KV_LORA), BF),
                   jax.ShapeDtypeStruct((Bd, N_HEADS * LANE), BF)],
        grid=(N_HEADS,),
        in_specs=[pl.BlockSpec((Bd, HEAD_W), lambda h: (0, h)),
                  pl.BlockSpec((None, QK_NOPE, KV_LORA), lambda h: (h, 0, 0))],
        out_specs=[pl.BlockSpec((Bd, KV_LORA), lambda h: (0, h)),
                   pl.BlockSpec((Bd, LANE), lambda h: (0, h))],
        compiler_params=_params("arbitrary"),
        name="absorb_q",
    )(q, w_ukt)


def _decode_kernel(pt_ref, qlat_ref, qpe_ref, ckvn_ref, kpen_ref, cckv_hbm, ckpe_hbm, o_ref,
                   ckv_buf, kpe_buf, sem, *, layer, n_pages, ch):
    b = pl.program_id(0)
    nb = pl.num_programs(0)
    n_ch = n_pages // ch

    def chunk_copies(bb, c, slot):
        cps = []
        for i in range(ch):
            pg = pt_ref[bb * n_pages + c * ch + i]
            cps.append(pltpu.make_async_copy(cckv_hbm.at[layer, pg], ckv_buf.at[slot, i], sem.at[0, slot]))
            cps.append(pltpu.make_async_copy(ckpe_hbm.at[layer, pg], kpe_buf.at[slot, i], sem.at[1, slot]))
        return cps

    def start(bb, c, slot):
        for cp in chunk_copies(bb, c, slot):
            cp.start()

    @pl.when(b == 0)
    def _():
        start(b, 0, 0)

    qlat = qlat_ref[...]
    qpe = qpe_ref[...][:, :QK_ROPE]
    ckvn = ckvn_ref[...].astype(BF).astype(F32)
    kpen = kpen_ref[...].astype(BF).astype(F32)
    m = (jnp.sum(qlat.astype(F32) * ckvn, axis=-1, keepdims=True)
         + jnp.sum(qpe.astype(F32) * kpen, axis=-1, keepdims=True))
    l = jnp.ones_like(m)
    acc = jnp.broadcast_to(ckvn, (N_HEADS, KV_LORA))

    for c in range(n_ch):
        slot = c % 2
        if c + 1 < n_ch:
            start(b, c + 1, 1 - slot)
        else:
            @pl.when(b + 1 < nb)
            def _():
                start(b + 1, 0, 1 - slot)
        for cp in chunk_copies(b, c, slot):
            cp.wait()
        kc = ckv_buf[slot].reshape(ch * ckv_buf.shape[2], KV_LORA).astype(BF)
        kp = kpe_buf[slot].reshape(ch * kpe_buf.shape[2], QK_ROPE).astype(BF)
        s = _dot_nt(qlat, kc) + _dot_nt(qpe, kp)
        m_new = jnp.maximum(m, jnp.max(s, axis=-1, keepdims=True))
        alpha = jnp.exp(m - m_new)
        p = jnp.exp(s - m_new)
        l = alpha * l + jnp.sum(p, axis=-1, keepdims=True)
        acc = alpha * acc + _dot(p.astype(BF), kc)
        m = m_new
    o_ref[...] = (acc / l).astype(o_ref.dtype)


def _paged_decode(page_table, qlat, qpe, ckv_new, kpe_new, cache_ckv, cache_kpe, layer, ch=8):
    Bd, n_pages = page_table.shape
    page = cache_ckv.shape[2]
    assert n_pages % (2 * ch) == 0
    grid_spec = pltpu.PrefetchScalarGridSpec(
        num_scalar_prefetch=1,
        grid=(Bd,),
        in_specs=[pl.BlockSpec((None, N_HEADS, KV_LORA), lambda b, pt: (b, 0, 0)),
                  pl.BlockSpec((None, N_HEADS, LANE), lambda b, pt: (b, 0, 0)),
                  pl.BlockSpec((None, 1, KV_LORA), lambda b, pt: (b, 0, 0)),
                  pl.BlockSpec((None, 1, QK_ROPE), lambda b, pt: (b, 0, 0)),
                  pl.BlockSpec(memory_space=pl.ANY),
                  pl.BlockSpec(memory_space=pl.ANY)],
        out_specs=pl.BlockSpec((None, N_HEADS, KV_LORA), lambda b, pt: (b, 0, 0)),
        scratch_shapes=[pltpu.VMEM((2, ch, page, KV_LORA), F32),
                        pltpu.VMEM((2, ch, page, QK_ROPE), F32),
                        pltpu.SemaphoreType.DMA((2, 2))],
    )
    return pl.pallas_call(
        functools.partial(_decode_kernel, layer=layer, n_pages=n_pages, ch=ch),
        out_shape=jax.ShapeDtypeStruct((Bd, N_HEADS, KV_LORA), BF),
        grid_spec=grid_spec,
        compiler_params=_params("arbitrary"),
        name="paged_decode",
    )(page_table.reshape(-1), qlat, qpe, ckv_new, kpe_new, cache_ckv, cache_kpe)


def _unabsorb_kernel(olat_ref, wuv_ref, o_ref):
    o_ref[...] = _dot(olat_ref[...], wuv_ref[...]).astype(o_ref.dtype)


def _unabsorb_o(olat, w_uv):
    Bd = olat.shape[0]
    return pl.pallas_call(
        _unabsorb_kernel,
        out_shape=jax.ShapeDtypeStruct((Bd, N_HEADS * V_DIM), BF),
        grid=(N_HEADS,),
        in_specs=[pl.BlockSpec((Bd, KV_LORA), lambda h: (0, h)),
                  pl.BlockSpec((KV_LORA, V_DIM), lambda h: (0, h))],
        out_specs=pl.BlockSpec((Bd, V_DIM), lambda h: (0, h)),
        compiler_params=_params("arbitrary"),
        name="unabsorb_o",
    )(olat, w_uv)


def _out_kernel(*refs, has_bias):
    if has_bias:
        a_ref, w_ref, b_ref, x_ref, gt_ref, o_ref = refs
    else:
        a_ref, w_ref, x_ref, gt_ref, o_ref = refs
    y = _dot(a_ref[...], w_ref[...])
    if has_bias:
        y = y + b_ref[...]
    o_ref[...] = x_ref[...] + gt_ref[...] * y


def _out_residual(grp, ts, a, w, bias, x, mod, gate_k, name):
    G, S = grp.G, grp.S
    K = a.shape[-1]
    has_bias = bias is not None
    in_specs = [_row_spec(ts, K), _const_spec(w.shape)]
    args = [a, w]
    if has_bias:
        in_specs.append(_const_spec((1, D_MODEL)))
        args.append(bias)
    in_specs += [_row_spec(ts, D_MODEL), _mod_spec(grp, ts, gate_k)]
    args += [x, mod]
    return pl.pallas_call(
        functools.partial(_out_kernel, has_bias=has_bias),
        out_shape=jax.ShapeDtypeStruct((G, S, D_MODEL), F32),
        grid=(G, S // ts), in_specs=in_specs, out_specs=_row_spec(ts, D_MODEL),
        compiler_params=_params("arbitrary", "arbitrary"),
        name=name,
    )(*args)


def _ffn_kernel(*refs, sample, final_norm, ts):
    if sample:
        (x_ref, s0_ref, s1_ref, sh_ref, sc_ref, gt_ref, ng_ref, wg_ref, wu_ref, wc_ref, bc_ref, wd_ref, fg_ref,
         o_ref, ns0_ref, ns1_ref, h_sc) = refs
    else:
        (x_ref, xh_ref, sh_ref, sc_ref, gt_ref, ng_ref, wg_ref, wu_ref, wc_ref, bc_ref, wd_ref, fg_ref,
         o_ref, gs_ref, h_sc, hh_sc) = refs
    t = pl.program_id(1)
    j = pl.program_id(2)
    nj = pl.num_programs(2)

    @pl.when(j == 0)
    def _():
        h_sc[...] = _modulated_norm(x_ref[...], ng_ref[...], sc_ref[...], sh_ref[...])
        if not sample:
            hh_sc[...] = _modulated_norm(xh_ref[...], ng_ref[...], sc_ref[...], sh_ref[...])
        o_ref[...] = jnp.zeros(o_ref.shape, F32)

    h = h_sc[...]
    g = _dot(h, wg_ref[...])
    u = _dot(h, wu_ref[...])
    wc = wc_ref[...]
    if sample:
        s1 = s1_ref[...]
        gc = wc[0:1] * s0_ref[...] + wc[1:2] * s1 + wc[2:3] * g + bc_ref[...]
        ns0_ref[...] = s1
        ns1_ref[...] = g
    else:
        gh = jnp.where(t > 0, _dot(hh_sc[...], wg_ref[...]), 0.0)
        gs_ref[...] = g[ts - 2:ts, :]
        row = lax.broadcasted_iota(jnp.int32, g.shape, 0)
        p1 = gh[HALO_FFN - 1:HALO_FFN]
        p2 = gh[HALO_FFN - 2:HALO_FFN - 1]
        g1 = jnp.where(row == 0, p1, pltpu.roll(g, 1, 0))
        g2 = jnp.where(row == 0, p2, jnp.where(row == 1, p1, pltpu.roll(g, 2, 0)))
        gc = wc[0:1] * g2 + wc[1:2] * g1 + wc[2:3] * g + bc_ref[...]
    a = (_silu(gc) * u).astype(BF)
    o_ref[...] += _dot(a, wd_ref[...])

    @pl.when(j == nj - 1)
    def _():
        y = x_ref[...] + gt_ref[...] * o_ref[...]
        if final_norm:
            y = _rms(y, fg_ref[...])
        o_ref[...] = y


def _conv_ffn(grp, ts, tf, x, mod, norm_g, w_gate, w_up, w_conv, b_conv, w_down, final_g, final_norm,
              state=None):
    G, S = grp.G, grp.S
    sample = state is not None
    nf = D_FF // tf
    nt = S // ts
    in_specs = [_row_spec(ts, D_MODEL)]
    args = [x]
    if sample:
        in_specs += [pl.BlockSpec((ts, tf), lambda g, t, j: (t, j)),
                     pl.BlockSpec((ts, tf), lambda g, t, j: (t, nf + j))]
        args += [state, state]
    else:
        hb = ts // HALO_FFN
        in_specs.append(pl.BlockSpec((None, HALO_FFN, D_MODEL),
                                     lambda g, t, j: (g, jnp.maximum(t * hb - 1, 0), 0)))
        args.append(x)
    in_specs += [_mod_spec(grp, ts, 3), _mod_spec(grp, ts, 4), _mod_spec(grp, ts, 5),
                 _const_spec((1, D_MODEL)),
                 pl.BlockSpec((D_MODEL, tf), lambda g, t, j: (0, j)),
                 pl.BlockSpec((D_MODEL, tf), lambda g, t, j: (0, j)),
                 pl.BlockSpec((3, tf), lambda g, t, j: (0, j)),
                 pl.BlockSpec((1, tf), lambda g, t, j: (0, j)),
                 pl.BlockSpec((tf, D_MODEL), lambda g, t, j: (j, 0)),
                 _const_spec((1, D_MODEL))]
    args += [mod, mod, mod, norm_g, w_gate, w_up, w_conv, b_conv, w_down, final_g]
    out_shape = [jax.ShapeDtypeStruct((G, S, D_MODEL), F32)]
    out_specs = [_row_spec(ts, D_MODEL)]
    scratch = [pltpu.VMEM((ts, D_MODEL), BF)]
    if sample:
        out_shape += [jax.ShapeDtypeStruct((S, D_FF), F32)] * 2
        out_specs += [pl.BlockSpec((ts, tf), lambda g, t, j: (t, j))] * 2
    else:
        out_shape.append(jax.ShapeDtypeStruct((G, nt, 2, D_FF), F32))
        out_specs.append(pl.BlockSpec((None, None, 2, tf), lambda g, t, j: (g, t, 0, j)))
        scratch.append(pltpu.VMEM((HALO_FFN, D_MODEL), BF))
    outs = pl.pallas_call(
        functools.partial(_ffn_kernel, sample=sample, final_norm=final_norm, ts=ts),
        out_shape=out_shape, grid=(G, nt, nf), in_specs=in_specs, out_specs=out_specs,
        scratch_shapes=scratch,
        compiler_params=_params("arbitrary", "arbitrary", "arbitrary"),
        name="conv_ffn_sample" if sample else "conv_ffn",
    )(*args)
    if sample:
        return outs[0], jnp.stack([outs[1], outs[2]], axis=1)
    return outs[0], outs[1][:, -1]


def _pw1_kernel(x_ref, sh_ref, sc_ref, ng_ref, wa_ref, wb_ref, ba_ref, bb_ref, u_ref, h_sc):
    @pl.when(pl.program_id(2) == 0)
    def _():
        h_sc[...] = _modulated_norm(x_ref[...], ng_ref[...], sc_ref[...], sh_ref[...])

    h = h_sc[...]
    a = _dot(h, wa_ref[...]) + ba_ref[...]
    b = _dot(h, wb_ref[...]) + bb_ref[...]
    u_ref[...] = a * jax.nn.sigmoid(b)


def _pw1_glu(grp, ts, tn, x, mod, norm_g, w_pw1, b_pw1):
    G, S = grp.G, grp.S
    nn = D_MODEL // tn
    return pl.pallas_call(
        _pw1_kernel,
        out_shape=jax.ShapeDtypeStruct((G, S, D_MODEL), F32),
        grid=(G, S // ts, nn),
        in_specs=[_row_spec(ts, D_MODEL), _mod_spec(grp, ts, 0), _mod_spec(grp, ts, 1),
                  _const_spec((1, D_MODEL)),
                  pl.BlockSpec((D_MODEL, tn), lambda g, t, j: (0, j)),
                  pl.BlockSpec((D_MODEL, tn), lambda g, t, j: (0, nn + j)),
                  pl.BlockSpec((1, tn), lambda g, t, j: (0, j)),
                  pl.BlockSpec((1, tn), lambda g, t, j: (0, nn + j))],
        out_specs=pl.BlockSpec((None, ts, tn), lambda g, t, j: (g, t, j)),
        scratch_shapes=[pltpu.VMEM((ts, D_MODEL), BF)],
        compiler_params=_params("arbitrary", "arbitrary", "arbitrary"),
        name="conv_pw1_glu",
    )(x, mod, mod, norm_g, w_pw1, w_pw1, b_pw1, b_pw1)


def _ln_swish(y, g, b):
    mu = jnp.mean(y, axis=-1, keepdims=True)
    yc = y - mu
    var = jnp.mean(yc * yc, axis=-1, keepdims=True)
    return _silu(yc * lax.rsqrt(var + EPS) * g + b).astype(BF)


def _dwconv_kernel(u_ref, uh_ref, w_ref, b_ref, lg_ref, lb_ref, y_ref, ext_sc, acc_sc, *, ts):
    t = pl.program_id(1)
    ext_sc[0:HALO_CONV, :] = jnp.where(t > 0, uh_ref[...], 0.0)
    ext_sc[HALO_CONV:, :] = u_ref[...]
    rc, cc = 32, 256
    lead = HALO_CONV - (CONV_W - 1)

    def body(r, carry):
        r0 = pl.multiple_of(r * rc, rc)
        for c in range(D_MODEL // cc):
            cols = slice(c * cc, (c + 1) * cc)
            win = ext_sc[pl.ds(r0, rc + HALO_CONV), cols]
            acc = jnp.zeros((rc, cc), F32) + b_ref[:, cols]
            for k in range(CONV_W):
                acc = acc + w_ref[k:k + 1, cols] * win[lead + k:lead + k + rc]
            acc_sc[pl.ds(r0, rc), cols] = acc
        return carry

    lax.fori_loop(0, ts // rc, body, 0)
    y_ref[...] = _ln_swish(acc_sc[...], lg_ref[...], lb_ref[...])


def _dwconv_prompt(ts, u, w_dw, b_dw, ln_g, ln_b):
    G, S, _ = u.shape
    hb = ts // HALO_CONV
    return pl.pallas_call(
        functools.partial(_dwconv_kernel, ts=ts),
        out_shape=jax.ShapeDtypeStruct((G, S, D_MODEL), BF),
        grid=(G, S // ts),
        in_specs=[_row_spec(ts, D_MODEL),
                  pl.BlockSpec((None, HALO_CONV, D_MODEL), lambda g, t: (g, jnp.maximum(t * hb - 1, 0), 0)),
                  _const_spec((CONV_W, D_MODEL)), _const_spec((1, D_MODEL)),
                  _const_spec((1, D_MODEL)), _const_spec((1, D_MODEL))],
        out_specs=_row_spec(ts, D_MODEL),
        scratch_shapes=[pltpu.VMEM((ts + HALO_CONV, D_MODEL), F32), pltpu.VMEM((ts, D_MODEL), F32)],
        compiler_params=_params("arbitrary", "arbitrary"),
        name="dwconv_ln_swish",
    )(u, u, w_dw, b_dw, ln_g, ln_b)


def _dwconv_s_kernel(st_ref, u_ref, w_ref, b_ref, lg_ref, lb_ref, y_ref, ns_ref):
    D = D_MODEL
    nprev = CONV_W - 1
    u = u_ref[...]
    acc = b_ref[...] + w_ref[nprev:nprev + 1, :] * u
    for k in range(nprev):
        acc = acc + w_ref[k:k + 1, :] * st_ref[:, k * D:(k + 1) * D]
    ns_ref[:, :(nprev - 1) * D] = st_ref[:, D:]
    ns_ref[:, (nprev - 1) * D:] = u
    y_ref[...] = _ln_swish(acc, lg_ref[...], lb_ref[...])


def _dwconv_sample(rs, state_flat, u, w_dw, b_dw, ln_g, ln_b):
    Bd, W = state_flat.shape
    return pl.pallas_call(
        _dwconv_s_kernel,
        out_shape=[jax.ShapeDtypeStruct((Bd, D_MODEL), BF), jax.ShapeDtypeStruct((Bd, W), F32)],
        grid=(Bd // rs,),
        in_specs=[pl.BlockSpec((rs, W), lambda i: (i, 0)), pl.BlockSpec((rs, D_MODEL), lambda i: (i, 0)),
                  _const_spec((CONV_W, D_MODEL)), _const_spec((1, D_MODEL)),
                  _const_spec((1, D_MODEL)), _const_spec((1, D_MODEL))],
        out_specs=[pl.BlockSpec((rs, D_MODEL), lambda i: (i, 0)), pl.BlockSpec((rs, W), lambda i: (i, 0))],
        compiler_params=_params("arbitrary"),
        name="dwconv_ln_swish_sample",
    )(state_flat, u, w_dw, b_dw, ln_g, ln_b)


def _rope_tables(pos):
    half = QK_ROPE // 2
    inv = ROPE_BASE ** (-jnp.arange(half, dtype=F32) / half)
    ang = pos.astype(F32)[:, None] * inv[None, :]
    cos, sin = jnp.cos(ang), jnp.sin(ang)
    z = jnp.zeros((pos.shape[0], LANE - QK_ROPE), F32)
    return jnp.concatenate([cos, cos, z], axis=1), jnp.concatenate([-sin, sin, z], axis=1)


def _swap_halves(w):
    half = QK_ROPE // 2
    return jnp.concatenate([w[..., half:], w[..., :half]], axis=-1)


def _mla_weights(wq_a, wq_b, wkv_a):
    z = jnp.zeros((D_MODEL, LANE - QK_ROPE), F32)
    pe = wkv_a[:, KV_LORA:]
    w_a = jnp.concatenate([wq_a, wkv_a[:, :KV_LORA], pe, z, _swap_halves(pe), z], axis=1).astype(BF)
    wqb = wq_b.reshape(Q_LORA, N_HEADS, QK_NOPE + QK_ROPE)
    nope, qpe = wqb[..., :QK_NOPE], wqb[..., QK_NOPE:]
    zq = jnp.zeros((Q_LORA, N_HEADS, LANE - QK_ROPE), F32)
    main = jnp.concatenate([nope, qpe, zq], axis=-1).reshape(Q_LORA, N_HEADS * HEAD_W)
    swapped = jnp.concatenate([_swap_halves(qpe), zq], axis=-1).reshape(Q_LORA, N_HEADS * LANE)
    return w_a, jnp.concatenate([main, swapped], axis=1).astype(BF)


def kernel(x_prompt, x_sample, cache_ckv, cache_kpe, state_conv, state_ffn, page_table, c_prompt, c_sample,
           w_mod, b_mod, norm_mix_g, norm_ffn_g, wq_a, q_norm_g, wq_b, wkv_a, kv_norm_g, w_uk, w_uv, w_o,
           conv_w_pw1, conv_b_pw1, conv_w_dw, conv_b_dw, conv_ln_g, conv_ln_b, conv_w_pw2, conv_b_pw2,
           ffn_w_gate, ffn_w_up, ffn_w_conv, ffn_b_conv, ffn_w_down, final_norm_g):
    D = D_MODEL
    B, S, _ = x_prompt.shape
    Bd = x_sample.shape[0]
    past_len = page_table.shape[1] * cache_ckv.shape[2]
    grp_p = Grp(B, S, False)
    grp_s = Grp(1, Bd, True)

    pad = (-(Bd + B)) % 8
    c_all = jnp.concatenate([c_sample, c_prompt, jnp.zeros((pad, D), F32)], axis=0)
    mod = _modulation(c_all, w_mod, b_mod)
    mod_s = [mod[i, :Bd][None] for i in range(2)]
    mod_p = [mod[i, Bd:Bd + B][:, None] for i in range(2)]

    row = lambda v: v.reshape(1, -1)
    xp = x_prompt
    xs = x_sample.reshape(1, Bd, D)

    w_a, wqb = _mla_weights(wq_a[0], wq_b[0], wkv_a[0])
    w_uk_flat = w_uk[0].reshape(KV_LORA, N_HEADS * QK_NOPE).astype(BF)
    w_uv_flat = w_uv[0].reshape(KV_LORA, N_HEADS * V_DIM).astype(BF)
    w_ukt = jnp.transpose(w_uk[0], (1, 2, 0)).astype(BF)
    w_o_b = w_o[0].astype(BF)
    cos_p, sin_p = _rope_tables(jnp.arange(S, dtype=jnp.int32))
    cos_s, sin_s = _rope_tables(jnp.full((Bd,), past_len, dtype=jnp.int32))
    ng, qg, kvg = row(norm_mix_g[0]), row(q_norm_g[0]), row(kv_norm_g[0])

    q_p, ckv_p, kpe_p, k_p, v_p = _mla_project(grp_p, 256, xp, mod_p[0], cos_p, sin_p, ng, w_a, qg, wqb, kvg,
                                               w_uk_flat, w_uv_flat)
    o_p = _prompt_attention(q_p, k_p, v_p, 512)
    xp = _out_residual(grp_p, 512, o_p, w_o_b, None, xp, mod_p[0], 2, "attn_out_residual")

    q_s, ckv_s, kpe_s = _mla_project(grp_s, Bd, xs, mod_s[0], cos_s, sin_s, ng, w_a, qg, wqb, kvg)
    qlat, qpe = _absorb_q(q_s[0], w_ukt)
    olat = _paged_decode(page_table, qlat.reshape(Bd, N_HEADS, KV_LORA), qpe.reshape(Bd, N_HEADS, LANE),
                         ckv_s.reshape(Bd, 1, KV_LORA), kpe_s.reshape(Bd, 1, QK_ROPE), cache_ckv, cache_kpe, 0)
    o_s = _unabsorb_o(olat.reshape(Bd, N_HEADS * KV_LORA), w_uv_flat)
    xs = _out_residual(grp_s, Bd, o_s[None], w_o_b, None, xs, mod_s[0], 2, "attn_out_residual_sample")

    fg = row(final_norm_g)

    def ffn(i, xp, xs, final_norm):
        wg, wu, wd = ffn_w_gate[i].astype(BF), ffn_w_up[i].astype(BF), ffn_w_down[i].astype(BF)
        wc, bc, ngf = ffn_w_conv[i], row(ffn_b_conv[i]), row(norm_ffn_g[i])
        xp, st_p = _conv_ffn(grp_p, 512, 512, xp, mod_p[i], ngf, wg, wu, wc, bc, wd, fg, final_norm)
        xs, st_s = _conv_ffn(grp_s, Bd, 512, xs, mod_s[i], ngf, wg, wu, wc, bc, wd, fg, final_norm,
                             state=state_ffn[i].reshape(Bd, 2 * D_FF))
        return xp, xs, st_p, st_s

    xp, xs, ffn_p0, ffn_s0 = ffn(0, xp, xs, False)

    w_pw1 = conv_w_pw1[0].astype(BF)
    b_pw1 = row(conv_b_pw1[0])
    w_pw2 = conv_w_pw2[0].astype(BF)
    b_pw2 = row(conv_b_pw2[0])
    ng1 = row(norm_mix_g[1])
    dw = (conv_w_dw[0], row(conv_b_dw[0]), row(conv_ln_g[0]), row(conv_ln_b[0]))

    u_p = _pw1_glu(grp_p, 1024, 512, xp, mod_p[1], ng1, w_pw1, b_pw1)
    y_p = _dwconv_prompt(256, u_p, *dw)
    xp = _out_residual(grp_p, 512, y_p, w_pw2, b_pw2, xp, mod_p[1], 2, "conv_out_residual")
    conv_p = u_p[:, S - (CONV_W - 1):]

    u_s = _pw1_glu(grp_s, Bd, 512, xs, mod_s[1], ng1, w_pw1, b_pw1)
    y_s, conv_s = _dwconv_sample(16, state_conv[0].reshape(Bd, (CONV_W - 1) * D), u_s[0], *dw)
    xs = _out_residual(grp_s, Bd, y_s[None], w_pw2, b_pw2, xs, mod_s[1], 2, "conv_out_residual_sample")

    yp, ys, ffn_p1, ffn_s1 = ffn(1, xp, xs, True)

    return (yp, ys.reshape(Bd, 1, D),
            ckv_p[None], kpe_p[None],
            ckv_s.reshape(1, Bd, 1, KV_LORA), kpe_s.reshape(1, Bd, 1, QK_ROPE),
            conv_p[None], conv_s.reshape(1, Bd, CONV_W - 1, D),
            jnp.stack([ffn_p0, ffn_p1]), jnp.stack([ffn_s0, ffn_s1]))
```

```python
import functools
import math
from typing import NamedTuple

import jax
import jax.numpy as jnp
from jax import lax
from jax.experimental import pallas as pl
from jax.experimental.pallas import tpu as pltpu

D_MODEL = 2048
N_HEADS = 16
Q_LORA = 512
KV_LORA = 512
QK_NOPE = 128
QK_ROPE = 64
V_DIM = 128
ATTN_SCALE = 1.0 / math.sqrt(QK_NOPE + QK_ROPE)
Q_SCALE = ATTN_SCALE * math.log2(math.e)
ROPE_BASE = 10000.0
CONV_W = 31
D_FF = 5632
EPS = 1e-6

HEAD_W = 256
LANE = 128
SUBLANE = 8
HALO_FFN = 8
HALO_CONV = 32
V7X_VMEM_LIMIT = 56 * 1024 * 1024

BF = jnp.bfloat16
F32 = jnp.float32


class Grp(NamedTuple):
    G: int
    S: int
    per_row: bool


def _params(*sem):
    return pltpu.CompilerParams(dimension_semantics=sem, vmem_limit_bytes=V7X_VMEM_LIMIT)


def _dot(a, b):
    return jnp.dot(a, b, preferred_element_type=F32)


def _dot_nt(a, b):
    return lax.dot_general(a, b, (((1,), (1,)), ((), ())), preferred_element_type=F32)


def _rms(x, g):
    return x * lax.rsqrt(jnp.mean(x * x, axis=-1, keepdims=True) + EPS) * g


def _silu(x):
    return x * jax.nn.sigmoid(x)


def _modulated_norm(x, g, sc, sh):
    return (_rms(x, g) * (1.0 + sc) + sh).astype(BF)


def _mod_spec(grp, ts, k):
    if grp.per_row:
        return pl.BlockSpec((None, ts, D_MODEL), lambda g, t, *_: (g, t, k))
    return pl.BlockSpec((None, 1, D_MODEL), lambda g, t, *_: (g, 0, k))


def _row_spec(ts, width):
    return pl.BlockSpec((None, ts, width), lambda g, t, *_: (g, t, 0))


def _const_spec(shape):
    nd = len(shape)
    return pl.BlockSpec(shape, lambda *_: (0,) * nd)


def _mod_kernel(c_ref, w_ref, b_ref, o_ref):
    c = c_ref[...]
    o_ref[...] = _dot(_silu(c).astype(BF), w_ref[...].astype(BF)) + b_ref[...]


def _modulation(c_all, w_mod, b_mod):
    L, D, N = w_mod.shape
    R = c_all.shape[0]
    tn = 1024
    return pl.pallas_call(
        _mod_kernel,
        out_shape=jax.ShapeDtypeStruct((L, R, N), F32),
        grid=(L, N // tn),
        in_specs=[pl.BlockSpec((R, D), lambda l, j: (0, 0)),
                  pl.BlockSpec((None, D, tn), lambda l, j: (l, 0, j)),
                  pl.BlockSpec((None, 1, tn), lambda l, j: (l, 0, j))],
        out_specs=pl.BlockSpec((None, R, tn), lambda l, j: (l, 0, j)),
        compiler_params=_params("arbitrary", "arbitrary"),
        name="modulation",
    )(c_all, w_mod, b_mod.reshape(L, 1, N))


def _proj_kernel(*refs, expand_kv):
    if expand_kv:
        (x_ref, sh_ref, sc_ref, cos_ref, sin_ref, ng_ref, wa_ref, qg_ref, wqb_ref, kvg_ref,
         wuk_ref, wuv_ref, q_ref, ckv_ref, kpe_ref, k_ref, v_ref) = refs
    else:
        (x_ref, sh_ref, sc_ref, cos_ref, sin_ref, ng_ref, wa_ref, qg_ref, wqb_ref, kvg_ref,
         q_ref, ckv_ref, kpe_ref) = refs
    h = _modulated_norm(x_ref[...], ng_ref[...], sc_ref[...], sh_ref[...])
    qkv = _dot(h, wa_ref[...])
    qn = _rms(qkv[:, :Q_LORA], qg_ref[...]).astype(BF)
    ckv = _rms(qkv[:, Q_LORA:Q_LORA + KV_LORA], kvg_ref[...])
    cos = cos_ref[...]
    sin = sin_ref[...]
    o = Q_LORA + KV_LORA
    kpe = qkv[:, o:o + LANE] * cos + qkv[:, o + LANE:o + 2 * LANE] * sin
    ckv_ref[...] = ckv
    kpe_ref[...] = kpe[:, :QK_ROPE]
    qm = _dot(qn, wqb_ref[...])
    sw0 = N_HEADS * HEAD_W
    for hd in range(N_HEADS):
        a = hd * HEAD_W
        q_ref[:, a:a + LANE] = (qm[:, a:a + LANE] * Q_SCALE).astype(BF)
        pe = qm[:, a + LANE:a + 2 * LANE] * cos + qm[:, sw0 + hd * LANE:sw0 + (hd + 1) * LANE] * sin
        q_ref[:, a + LANE:a + 2 * LANE] = (pe * Q_SCALE).astype(BF)
    if expand_kv:
        cb = ckv.astype(BF)
        kn = _dot(cb, wuk_ref[...])
        kpe_b = kpe.astype(BF)
        for hd in range(N_HEADS):
            a = hd * HEAD_W
            k_ref[:, a:a + LANE] = kn[:, hd * LANE:(hd + 1) * LANE].astype(BF)
            k_ref[:, a + LANE:a + 2 * LANE] = kpe_b
        v_ref[...] = _dot(cb, wuv_ref[...]).astype(BF)


def _mla_project(grp, ts, x, mod, cos, sin, norm_g, w_a, q_norm_g, wqb, kv_norm_g, w_uk=None, w_uv=None):
    expand_kv = w_uk is not None
    G, S = grp.G, grp.S
    qw = N_HEADS * HEAD_W
    in_specs = [_row_spec(ts, D_MODEL), _mod_spec(grp, ts, 0), _mod_spec(grp, ts, 1),
                pl.BlockSpec((ts, LANE), lambda g, t: (t, 0)), pl.BlockSpec((ts, LANE), lambda g, t: (t, 0)),
                _const_spec((1, D_MODEL)), _const_spec(w_a.shape), _const_spec((1, Q_LORA)),
                _const_spec(wqb.shape), _const_spec((1, KV_LORA))]
    args = [x, mod, mod, cos, sin, norm_g, w_a, q_norm_g, wqb, kv_norm_g]
    out_shape = [jax.ShapeDtypeStruct((G, S, qw), BF), jax.ShapeDtypeStruct((G, S, KV_LORA), F32),
                 jax.ShapeDtypeStruct((G, S, QK_ROPE), F32)]
    out_specs = [_row_spec(ts, qw), _row_spec(ts, KV_LORA), _row_spec(ts, QK_ROPE)]
    if expand_kv:
        in_specs += [_const_spec(w_uk.shape), _const_spec(w_uv.shape)]
        args += [w_uk, w_uv]
        out_shape += [jax.ShapeDtypeStruct((G, S, qw), BF), jax.ShapeDtypeStruct((G, S, N_HEADS * V_DIM), BF)]
        out_specs += [_row_spec(ts, qw), _row_spec(ts, N_HEADS * V_DIM)]
    return pl.pallas_call(
        functools.partial(_proj_kernel, expand_kv=expand_kv),
        out_shape=out_shape, grid=(G, S // ts), in_specs=in_specs, out_specs=out_specs,
        compiler_params=_params("arbitrary", "arbitrary"),
        name="mla_project_kv" if expand_kv else "mla_project",
    )(*args)


def _attn_kernel(q_ref, k_ref, v_ref, o_ref, *, tq, seq):
    row = lax.broadcasted_iota(jnp.int32, (tq, tq), 0)
    col = lax.broadcasted_iota(jnp.int32, (tq, tq), 1)
    causal = col <= row
    for i in range(seq // tq):
        lo, hi = i * tq, (i + 1) * tq
        q = q_ref[lo:hi, :]
        sd = jnp.where(causal, _dot_nt(q, k_ref[lo:hi, :]), -jnp.inf)
        m = jnp.max(sd, axis=-1, keepdims=True)
        if i > 0:
            so = _dot_nt(q, k_ref[0:lo, :])
            m = jnp.maximum(m, jnp.max(so, axis=-1, keepdims=True))
        pd = jnp.exp2(sd - m)
        l = jnp.sum(pd, axis=-1, keepdims=True)
        acc = _dot(pd.astype(BF), v_ref[lo:hi, :])
        if i > 0:
            po = jnp.exp2(so - m)
            l = l + jnp.sum(po, axis=-1, keepdims=True)
            acc = acc + _dot(po.astype(BF), v_ref[0:lo, :])
        o_ref[lo:hi, :] = (acc / l).astype(o_ref.dtype)


def _prompt_attention(q, k, v, tq):
    B, S, _ = q.shape
    return pl.pallas_call(
        functools.partial(_attn_kernel, tq=tq, seq=S),
        out_shape=jax.ShapeDtypeStruct((B, S, N_HEADS * V_DIM), BF),
        grid=(B, N_HEADS),
        in_specs=[pl.BlockSpec((None, S, HEAD_W), lambda b, h: (b, 0, h)),
                  pl.BlockSpec((None, S, HEAD_W), lambda b, h: (b, 0, h)),
                  pl.BlockSpec((None, S, V_DIM), lambda b, h: (b, 0, h))],
        out_specs=pl.BlockSpec((None, S, V_DIM), lambda b, h: (b, 0, h)),
        compiler_params=_params("arbitrary", "arbitrary"),
        name="prompt_attention",
    )(q, k, v)


def _absorb_kernel(q_ref, wukt_ref, qlat_ref, qpe_ref):
    q = q_ref[...]
    qlat_ref[...] = _dot(q[:, :QK_NOPE], wukt_ref[...]).astype(BF)
    qpe_ref[...] = q[:, QK_NOPE:]


def _absorb_q(q, w_ukt):
    Bd = q.shape[0]
    return pl.pallas_call(
        _absorb_kernel,
        out_shape=[jax.ShapeDtypeStruct((Bd, N_HEADS * KV_LORA), BF),
                   jax.ShapeDtypeStruct((Bd, N_HEADS * LANE), BF)],
        grid=(N_HEADS,),
        in_specs=[pl.BlockSpec((Bd, HEAD_W), lambda h: (0, h)),
                  pl.BlockSpec((None, QK_NOPE, KV_LORA), lambda h: (h, 0, 0))],
        out_specs=[pl.BlockSpec((Bd, KV_LORA), lambda h: (0, h)),
                   pl.BlockSpec((Bd, LANE), lambda h: (0, h))],
        compiler_params=_params("arbitrary"),
        name="absorb_q",
    )(q, w_ukt)


DEC_SLOTS = 4
DEC_AHEAD = 2


def _decode_kernel(pt_ref, qlat_ref, qpe_ref, ckvn_ref, kpen_ref, cckv_hbm, ckpet_hbm, o_ref,
                   ckv_buf, kpe_buf, kcb, sem, *, layer, n_pages, ch):
    b = pl.program_id(0)
    nb = pl.num_programs(0)
    n_ch = n_pages // ch
    page = ckv_buf.shape[2]

    def chunk_copies(bb, c):
        slot = c % DEC_SLOTS
        cps = []
        for i in range(ch):
            pg = pt_ref[bb * n_pages + c * ch + i]
            cps.append(pltpu.make_async_copy(cckv_hbm.at[layer, pg], ckv_buf.at[slot, i], sem.at[0, slot]))
            cps.append(pltpu.make_async_copy(ckpet_hbm.at[layer, pg], kpe_buf.at[slot, i], sem.at[1, slot]))
        return cps

    def start(bb, c):
        for cp in chunk_copies(bb, c):
            cp.start()

    def start_ahead(c):
        if c < n_ch:
            start(b, c)
        else:
            @pl.when(b + 1 < nb)
            def _():
                start(b + 1, c - n_ch)

    @pl.when(b == 0)
    def _():
        for c in range(DEC_AHEAD):
            start(b, c)

    qlat = qlat_ref[...]
    qpe = qpe_ref[...][:, :QK_ROPE]
    ckvn = ckvn_ref[...].astype(BF).astype(F32)
    kpen = kpen_ref[...].astype(BF).astype(F32)

    def scores(c):
        for cp in chunk_copies(b, c):
            cp.wait()
        slot = c % DEC_SLOTS
        kc = ckv_buf[slot].reshape(ch * page, KV_LORA).astype(BF)
        kcb[c % 2] = kc
        pe = [_dot(qpe, kpe_buf[slot, i].astype(BF)) for i in range(ch)]
        return _dot_nt(qlat, kc) + jnp.concatenate(pe, axis=1)

    m = (jnp.sum(qlat.astype(F32) * ckvn, axis=-1, keepdims=True)
         + jnp.sum(qpe.astype(F32) * kpen, axis=-1, keepdims=True))
    l = jnp.ones_like(m)
    acc = jnp.broadcast_to(ckvn, (N_HEADS, KV_LORA))

    s = scores(0)
    for c in range(n_ch):
        start_ahead(c + DEC_AHEAD)
        s_next = scores(c + 1) if c + 1 < n_ch else None
        m_new = jnp.maximum(m, jnp.max(s, axis=-1, keepdims=True))
        alpha = jnp.exp2(m - m_new)
        p = jnp.exp2(s - m_new)
        l = alpha * l + jnp.sum(p, axis=-1, keepdims=True)
        acc = alpha * acc + _dot(p.astype(BF), kcb[c % 2])
        m = m_new
        s = s_next
    o_ref[...] = (acc / l).astype(o_ref.dtype)


def _paged_decode(page_table, qlat, qpe, ckv_new, kpe_new, cache_ckv, cache_kpet, layer, ch=8):
    Bd, n_pages = page_table.shape
    page = cache_ckv.shape[2]
    n_ch = n_pages // ch
    assert n_pages % ch == 0 and n_ch % DEC_SLOTS == 0 and n_ch >= DEC_AHEAD
    grid_spec = pltpu.PrefetchScalarGridSpec(
        num_scalar_prefetch=1,
        grid=(Bd,),
        in_specs=[pl.BlockSpec((None, N_HEADS, KV_LORA), lambda b, pt: (b, 0, 0)),
                  pl.BlockSpec((None, N_HEADS, LANE), lambda b, pt: (b, 0, 0)),
                  pl.BlockSpec((None, 1, KV_LORA), lambda b, pt: (b, 0, 0)),
                  pl.BlockSpec((None, 1, QK_ROPE), lambda b, pt: (b, 0, 0)),
                  pl.BlockSpec(memory_space=pl.ANY),
                  pl.BlockSpec(memory_space=pl.ANY)],
        out_specs=pl.BlockSpec((None, N_HEADS, KV_LORA), lambda b, pt: (b, 0, 0)),
        scratch_shapes=[pltpu.VMEM((DEC_SLOTS, ch, page, KV_LORA), F32),
                        pltpu.VMEM((DEC_SLOTS, ch, QK_ROPE, page), F32),
                        pltpu.VMEM((2, ch * page, KV_LORA), BF),
                        pltpu.SemaphoreType.DMA((2, DEC_SLOTS))],
    )
    return pl.pallas_call(
        functools.partial(_decode_kernel, layer=layer, n_pages=n_pages, ch=ch),
        out_shape=jax.ShapeDtypeStruct((Bd, N_HEADS, KV_LORA), BF),
        grid_spec=grid_spec,
        compiler_params=_params("arbitrary"),
        name="paged_decode",
    )(page_table.reshape(-1), qlat, qpe, ckv_new, kpe_new, cache_ckv, cache_kpet)


def _unabsorb_kernel(olat_ref, wuv_ref, o_ref):
    o_ref[...] = _dot(olat_ref[...], wuv_ref[...]).astype(o_ref.dtype)


def _unabsorb_o(olat, w_uv):
    Bd = olat.shape[0]
    return pl.pallas_call(
        _unabsorb_kernel,
        out_shape=jax.ShapeDtypeStruct((Bd, N_HEADS * V_DIM), BF),
        grid=(N_HEADS,),
        in_specs=[pl.BlockSpec((Bd, KV_LORA), lambda h: (0, h)),
                  pl.BlockSpec((KV_LORA, V_DIM), lambda h: (0, h))],
        out_specs=pl.BlockSpec((Bd, V_DIM), lambda h: (0, h)),
        compiler_params=_params("arbitrary"),
        name="unabsorb_o",
    )(olat, w_uv)


def _out_kernel(*refs, has_bias):
    if has_bias:
        a_ref, w_ref, b_ref, x_ref, gt_ref, o_ref = refs
    else:
        a_ref, w_ref, x_ref, gt_ref, o_ref = refs
    y = _dot(a_ref[...], w_ref[...])
    if has_bias:
        y = y + b_ref[...]
    o_ref[...] = x_ref[...] + gt_ref[...] * y


def _out_residual(grp, ts, a, w, bias, x, mod, gate_k, name):
    G, S = grp.G, grp.S
    K = a.shape[-1]
    has_bias = bias is not None
    in_specs = [_row_spec(ts, K), _const_spec(w.shape)]
    args = [a, w]
    if has_bias:
        in_specs.append(_const_spec((1, D_MODEL)))
        args.append(bias)
    in_specs += [_row_spec(ts, D_MODEL), _mod_spec(grp, ts, gate_k)]
    args += [x, mod]
    return pl.pallas_call(
        functools.partial(_out_kernel, has_bias=has_bias),
        out_shape=jax.ShapeDtypeStruct((G, S, D_MODEL), F32),
        grid=(G, S // ts), in_specs=in_specs, out_specs=_row_spec(ts, D_MODEL),
        compiler_params=_params("arbitrary", "arbitrary"),
        name=name,
    )(*args)


def _ffn_kernel(*refs, sample, final_norm, ts):
    if sample:
        (x_ref, s0_ref, s1_ref, sh_ref, sc_ref, gt_ref, ng_ref, wg_ref, wu_ref, wc_ref, bc_ref, wd_ref, fg_ref,
         o_ref, ns0_ref, ns1_ref, h_sc) = refs
    else:
        (x_ref, xh_ref, sh_ref, sc_ref, gt_ref, ng_ref, wg_ref, wu_ref, wc_ref, bc_ref, wd_ref, fg_ref,
         o_ref, gs_ref, h_sc, hh_sc) = refs
    t = pl.program_id(1)
    j = pl.program_id(2)
    nj = pl.num_programs(2)

    @pl.when(j == 0)
    def _():
        h_sc[...] = _modulated_norm(x_ref[...], ng_ref[...], sc_ref[...], sh_ref[...])
        if not sample:
            hh_sc[...] = _modulated_norm(xh_ref[...], ng_ref[...], sc_ref[...], sh_ref[...])
        o_ref[...] = jnp.zeros(o_ref.shape, F32)

    h = h_sc[...]
    g = _dot(h, wg_ref[...])
    u = _dot(h, wu_ref[...])
    wc = wc_ref[...]
    if sample:
        s1 = s1_ref[...]
        gc = wc[0:1] * s0_ref[...] + wc[1:2] * s1 + wc[2:3] * g + bc_ref[...]
        ns0_ref[...] = s1
        ns1_ref[...] = g
    else:
        gh = jnp.where(t > 0, _dot(hh_sc[...], wg_ref[...]), 0.0)
        gs_ref[...] = g[ts - 2:ts, :]
        row = lax.broadcasted_iota(jnp.int32, g.shape, 0)
        p1 = gh[HALO_FFN - 1:HALO_FFN]
        p2 = gh[HALO_FFN - 2:HALO_FFN - 1]
        g1 = jnp.where(row == 0, p1, pltpu.roll(g, 1, 0))
        g2 = jnp.where(row == 0, p2, jnp.where(row == 1, p1, pltpu.roll(g, 2, 0)))
        gc = wc[0:1] * g2 + wc[1:2] * g1 + wc[2:3] * g + bc_ref[...]
    a = (_silu(gc) * u).astype(BF)
    o_ref[...] += _dot(a, wd_ref[...])

    @pl.when(j == nj - 1)
    def _():
        y = x_ref[...] + gt_ref[...] * o_ref[...]
        if final_norm:
            y = _rms(y, fg_ref[...])
        o_ref[...] = y


def _conv_ffn(grp, ts, tf, layer, x, mod, norm_g, w_gate, w_up, w_conv, b_conv, w_down, final_g, final_norm,
              state=None):
    G, S = grp.G, grp.S
    sample = state is not None
    nf = D_FF // tf
    nt = S // ts
    in_specs = [_row_spec(ts, D_MODEL)]
    args = [x]
    if sample:
        in_specs += [pl.BlockSpec((ts, tf), lambda g, t, j: (t, j)),
                     pl.BlockSpec((ts, tf), lambda g, t, j: (t, nf + j))]
        args += [state, state]
    else:
        hb = ts // HALO_FFN
        in_specs.append(pl.BlockSpec((None, HALO_FFN, D_MODEL),
                                     lambda g, t, j: (g, jnp.maximum(t * hb - 1, 0), 0)))
        args.append(x)
    in_specs += [_mod_spec(grp, ts, 3), _mod_spec(grp, ts, 4), _mod_spec(grp, ts, 5),
                 _const_spec((1, D_MODEL)),
                 pl.BlockSpec((None, D_MODEL, tf), lambda g, t, j: (layer, 0, j)),
                 pl.BlockSpec((None, D_MODEL, tf), lambda g, t, j: (layer, 0, j)),
                 pl.BlockSpec((None, 3, tf), lambda g, t, j: (layer, 0, j)),
                 pl.BlockSpec((None, 1, tf), lambda g, t, j: (layer, 0, j)),
                 pl.BlockSpec((None, tf, D_MODEL), lambda g, t, j: (layer, j, 0)),
                 _const_spec((1, D_MODEL))]
    args += [mod, mod, mod, norm_g, w_gate, w_up, w_conv, b_conv, w_down, final_g]
    out_shape = [jax.ShapeDtypeStruct((G, S, D_MODEL), F32)]
    out_specs = [_row_spec(ts, D_MODEL)]
    scratch = [pltpu.VMEM((ts, D_MODEL), BF)]
    if sample:
        out_shape += [jax.ShapeDtypeStruct((S, D_FF), F32)] * 2
        out_specs += [pl.BlockSpec((ts, tf), lambda g, t, j: (t, j))] * 2
    else:
        out_shape.append(jax.ShapeDtypeStruct((G, nt, 2, D_FF), F32))
        out_specs.append(pl.BlockSpec((None, None, 2, tf), lambda g, t, j: (g, t, 0, j)))
        scratch.append(pltpu.VMEM((HALO_FFN, D_MODEL), BF))
    outs = pl.pallas_call(
        functools.partial(_ffn_kernel, sample=sample, final_norm=final_norm, ts=ts),
        out_shape=out_shape, grid=(G, nt, nf), in_specs=in_specs, out_specs=out_specs,
        scratch_shapes=scratch,
        compiler_params=_params("arbitrary", "arbitrary", "arbitrary"),
        name="conv_ffn_sample" if sample else "conv_ffn",
    )(*args)
    if sample:
        return outs[0], jnp.stack([outs[1], outs[2]], axis=1)
    return outs[0], outs[1][:, -1]


def _pw1_kernel(x_ref, sh_ref, sc_ref, ng_ref, wa_ref, wb_ref, ba_ref, bb_ref, u_ref, h_sc):
    @pl.when(pl.program_id(2) == 0)
    def _():
        h_sc[...] = _modulated_norm(x_ref[...], ng_ref[...], sc_ref[...], sh_ref[...])

    h = h_sc[...]
    a = _dot(h, wa_ref[...]) + ba_ref[...]
    b = _dot(h, wb_ref[...]) + bb_ref[...]
    u_ref[...] = a * jax.nn.sigmoid(b)


def _pw1_glu(grp, ts, tn, x, mod, norm_g, w_pw1, b_pw1):
    G, S = grp.G, grp.S
    nn = D_MODEL // tn
    return pl.pallas_call(
        _pw1_kernel,
        out_shape=jax.ShapeDtypeStruct((G, S, D_MODEL), F32),
        grid=(G, S // ts, nn),
        in_specs=[_row_spec(ts, D_MODEL), _mod_spec(grp, ts, 0), _mod_spec(grp, ts, 1),
                  _const_spec((1, D_MODEL)),
                  pl.BlockSpec((D_MODEL, tn), lambda g, t, j: (0, j)),
                  pl.BlockSpec((D_MODEL, tn), lambda g, t, j: (0, nn + j)),
                  pl.BlockSpec((1, tn), lambda g, t, j: (0, j)),
                  pl.BlockSpec((1, tn), lambda g, t, j: (0, nn + j))],
        out_specs=pl.BlockSpec((None, ts, tn), lambda g, t, j: (g, t, j)),
        scratch_shapes=[pltpu.VMEM((ts, D_MODEL), BF)],
        compiler_params=_params("arbitrary", "arbitrary", "arbitrary"),
        name="conv_pw1_glu",
    )(x, mod, mod, norm_g, w_pw1, w_pw1, b_pw1, b_pw1)


def _ln_swish(y, g, b):
    mu = jnp.mean(y, axis=-1, keepdims=True)
    yc = y - mu
    var = jnp.mean(yc * yc, axis=-1, keepdims=True)
    return _silu(yc * lax.rsqrt(var + EPS) * g + b).astype(BF)


def _dwconv_kernel(u_ref, uh_ref, w_ref, b_ref, lg_ref, lb_ref, y_ref, ext_sc, sh_sc, acc_sc, *, ts):
    t = pl.program_id(1)
    ext_sc[0:HALO_CONV, :] = jnp.where(t > 0, uh_ref[...], 0.0)
    ext_sc[HALO_CONV:, :] = u_ref[...]
    rows = ts + HALO_CONV
    rc = 64
    lead = HALO_CONV - (CONV_W - 1)

    for c in range(D_MODEL // LANE):
        cols = slice(c * LANE, (c + 1) * LANE)
        x = ext_sc[:, cols]
        for r in range(1, SUBLANE):
            sh_sc[r] = pltpu.roll(x, rows - r, 0)

        def body(i, carry):
            r0 = pl.multiple_of(i * rc, rc)
            acc = jnp.zeros((rc, LANE), F32) + b_ref[:, cols]
            for k in range(CONV_W):
                a, r = divmod(lead + k, SUBLANE)
                if r == 0:
                    win = ext_sc[pl.ds(r0 + SUBLANE * a, rc), cols]
                else:
                    win = sh_sc[r, pl.ds(r0 + SUBLANE * a, rc), :]
                acc = acc + w_ref[k:k + 1, cols] * win
            acc_sc[pl.ds(r0, rc), cols] = acc
            return carry

        lax.fori_loop(0, ts // rc, body, 0)
    y_ref[...] = _ln_swish(acc_sc[...], lg_ref[...], lb_ref[...])


def _dwconv_prompt(ts, u, w_dw, b_dw, ln_g, ln_b):
    G, S, _ = u.shape
    hb = ts // HALO_CONV
    return pl.pallas_call(
        functools.partial(_dwconv_kernel, ts=ts),
        out_shape=jax.ShapeDtypeStruct((G, S, D_MODEL), BF),
        grid=(G, S // ts),
        in_specs=[_row_spec(ts, D_MODEL),
                  pl.BlockSpec((None, HALO_CONV, D_MODEL), lambda g, t: (g, jnp.maximum(t * hb - 1, 0), 0)),
                  _const_spec((CONV_W, D_MODEL)), _const_spec((1, D_MODEL)),
                  _const_spec((1, D_MODEL)), _const_spec((1, D_MODEL))],
        out_specs=_row_spec(ts, D_MODEL),
        scratch_shapes=[pltpu.VMEM((ts + HALO_CONV, D_MODEL), F32),
                        pltpu.VMEM((SUBLANE, ts + HALO_CONV, LANE), F32),
                        pltpu.VMEM((ts, D_MODEL), F32)],
        compiler_params=_params("arbitrary", "arbitrary"),
        name="dwconv_ln_swish",
    )(u, u, w_dw, b_dw, ln_g, ln_b)


def _dwconv_s_kernel(st_ref, u_ref, w_ref, b_ref, lg_ref, lb_ref, y_ref, ns_ref):
    nprev = CONV_W - 1
    u = u_ref[...]
    acc = b_ref[...] + w_ref[nprev:nprev + 1, :] * u
    for k in range(nprev):
        acc = acc + w_ref[k:k + 1, :] * st_ref[k]
    for k in range(nprev - 1):
        ns_ref[k] = st_ref[k + 1]
    ns_ref[nprev - 1] = u
    y_ref[...] = _ln_swish(acc, lg_ref[...], lb_ref[...])


def _dwconv_sample(rs, state_t, u, w_dw, b_dw, ln_g, ln_b):
    nprev, Bd, _ = state_t.shape
    st_spec = pl.BlockSpec((nprev, rs, D_MODEL), lambda i: (0, i, 0))
    return pl.pallas_call(
        _dwconv_s_kernel,
        out_shape=[jax.ShapeDtypeStruct((Bd, D_MODEL), BF), jax.ShapeDtypeStruct(state_t.shape, F32)],
        grid=(Bd // rs,),
        in_specs=[st_spec, pl.BlockSpec((rs, D_MODEL), lambda i: (i, 0)),
                  _const_spec((CONV_W, D_MODEL)), _const_spec((1, D_MODEL)),
                  _const_spec((1, D_MODEL)), _const_spec((1, D_MODEL))],
        out_specs=[pl.BlockSpec((rs, D_MODEL), lambda i: (i, 0)), st_spec],
        compiler_params=_params("arbitrary"),
        name="dwconv_ln_swish_sample",
    )(state_t, u, w_dw, b_dw, ln_g, ln_b)


def _rope_tables(pos):
    half = QK_ROPE // 2
    inv = ROPE_BASE ** (-jnp.arange(half, dtype=F32) / half)
    ang = pos.astype(F32)[:, None] * inv[None, :]
    cos, sin = jnp.cos(ang), jnp.sin(ang)
    z = jnp.zeros((pos.shape[0], LANE - QK_ROPE), F32)
    return jnp.concatenate([cos, cos, z], axis=1), jnp.concatenate([-sin, sin, z], axis=1)


def _swap_halves(w):
    half = QK_ROPE // 2
    return jnp.concatenate([w[..., half:], w[..., :half]], axis=-1)


def _mla_weights(wq_a, wq_b, wkv_a):
    z = jnp.zeros((D_MODEL, LANE - QK_ROPE), F32)
    pe = wkv_a[:, KV_LORA:]
    w_a = jnp.concatenate([wq_a, wkv_a[:, :KV_LORA], pe, z, _swap_halves(pe), z], axis=1).astype(BF)
    wqb = wq_b.reshape(Q_LORA, N_HEADS, QK_NOPE + QK_ROPE)
    nope, qpe = wqb[..., :QK_NOPE], wqb[..., QK_NOPE:]
    zq = jnp.zeros((Q_LORA, N_HEADS, LANE - QK_ROPE), F32)
    main = jnp.concatenate([nope, qpe, zq], axis=-1).reshape(Q_LORA, N_HEADS * HEAD_W)
    swapped = jnp.concatenate([_swap_halves(qpe), zq], axis=-1).reshape(Q_LORA, N_HEADS * LANE)
    return w_a, jnp.concatenate([main, swapped], axis=1).astype(BF)


def kernel(x_prompt, x_sample, cache_ckv, cache_kpe, state_conv, state_ffn, page_table, c_prompt, c_sample,
           w_mod, b_mod, norm_mix_g, norm_ffn_g, wq_a, q_norm_g, wq_b, wkv_a, kv_norm_g, w_uk, w_uv, w_o,
           conv_w_pw1, conv_b_pw1, conv_w_dw, conv_b_dw, conv_ln_g, conv_ln_b, conv_w_pw2, conv_b_pw2,
           ffn_w_gate, ffn_w_up, ffn_w_conv, ffn_b_conv, ffn_w_down, final_norm_g):
    D = D_MODEL
    B, S, _ = x_prompt.shape
    Bd = x_sample.shape[0]
    past_len = page_table.shape[1] * cache_ckv.shape[2]
    grp_p = Grp(B, S, False)
    grp_s = Grp(1, Bd, True)

    pad = (-(Bd + B)) % 8
    c_all = jnp.concatenate([c_sample, c_prompt, jnp.zeros((pad, D), F32)], axis=0)
    mod = _modulation(c_all, w_mod, b_mod)
    mod_s = [mod[i, :Bd][None] for i in range(2)]
    mod_p = [mod[i, Bd:Bd + B][:, None] for i in range(2)]

    row = lambda v: v.reshape(1, -1)
    xp = x_prompt
    xs = x_sample.reshape(1, Bd, D)

    w_a, wqb = _mla_weights(wq_a[0], wq_b[0], wkv_a[0])
    w_uk_flat = w_uk[0].reshape(KV_LORA, N_HEADS * QK_NOPE).astype(BF)
    w_uv_flat = w_uv[0].reshape(KV_LORA, N_HEADS * V_DIM).astype(BF)
    w_ukt = jnp.transpose(w_uk[0], (1, 2, 0)).astype(BF)
    w_o_b = w_o[0].astype(BF)
    cos_p, sin_p = _rope_tables(jnp.arange(S, dtype=jnp.int32))
    cos_s, sin_s = _rope_tables(jnp.full((Bd,), past_len, dtype=jnp.int32))
    ng, qg, kvg = row(norm_mix_g[0]), row(q_norm_g[0]), row(kv_norm_g[0])

    q_p, ckv_p, kpe_p, k_p, v_p = _mla_project(grp_p, 256, xp, mod_p[0], cos_p, sin_p, ng, w_a, qg, wqb, kvg,
                                               w_uk_flat, w_uv_flat)
    o_p = _prompt_attention(q_p, k_p, v_p, 256)
    xp = _out_residual(grp_p, 512, o_p, w_o_b, None, xp, mod_p[0], 2, "attn_out_residual")

    q_s, ckv_s, kpe_s = _mla_project(grp_s, Bd, xs, mod_s[0], cos_s, sin_s, ng, w_a, qg, wqb, kvg)
    qlat, qpe = _absorb_q(q_s[0], w_ukt)
    cache_kpet = jnp.swapaxes(cache_kpe, 2, 3)
    olat = _paged_decode(page_table, qlat.reshape(Bd, N_HEADS, KV_LORA), qpe.reshape(Bd, N_HEADS, LANE),
                         ckv_s.reshape(Bd, 1, KV_LORA), kpe_s.reshape(Bd, 1, QK_ROPE), cache_ckv, cache_kpet, 0)
    o_s = _unabsorb_o(olat.reshape(Bd, N_HEADS * KV_LORA), w_uv_flat)
    xs = _out_residual(grp_s, Bd, o_s[None], w_o_b, None, xs, mod_s[0], 2, "attn_out_residual_sample")

    fg = row(final_norm_g)
    wg, wu, wd = ffn_w_gate.astype(BF), ffn_w_up.astype(BF), ffn_w_down.astype(BF)
    bc = ffn_b_conv[:, None, :]

    def ffn(i, xp, xs, final_norm):
        ngf = row(norm_ffn_g[i])
        xp, st_p = _conv_ffn(grp_p, 512, 512, i, xp, mod_p[i], ngf, wg, wu, ffn_w_conv, bc, wd, fg, final_norm)
        xs, st_s = _conv_ffn(grp_s, Bd, 512, i, xs, mod_s[i], ngf, wg, wu, ffn_w_conv, bc, wd, fg, final_norm,
                             state=state_ffn[i].reshape(Bd, 2 * D_FF))
        return xp, xs, st_p, st_s

    xp, xs, ffn_p0, ffn_s0 = ffn(0, xp, xs, False)

    w_pw1 = conv_w_pw1[0].astype(BF)
    b_pw1 = row(conv_b_pw1[0])
    w_pw2 = conv_w_pw2[0].astype(BF)
    b_pw2 = row(conv_b_pw2[0])
    ng1 = row(norm_mix_g[1])
    dw = (conv_w_dw[0], row(conv_b_dw[0]), row(conv_ln_g[0]), row(conv_ln_b[0]))

    u_p = _pw1_glu(grp_p, 1024, 512, xp, mod_p[1], ng1, w_pw1, b_pw1)
    y_p = _dwconv_prompt(256, u_p, *dw)
    xp = _out_residual(grp_p, 512, y_p, w_pw2, b_pw2, xp, mod_p[1], 2, "conv_out_residual")
    conv_p = u_p[:, S - (CONV_W - 1):]

    u_s = _pw1_glu(grp_s, Bd, 512, xs, mod_s[1], ng1, w_pw1, b_pw1)
    y_s, conv_s = _dwconv_sample(16, jnp.swapaxes(state_conv[0], 0, 1), u_s[0], *dw)
    xs = _out_residual(grp_s, Bd, y_s[None], w_pw2, b_pw2, xs, mod_s[1], 2, "conv_out_residual_sample")

    yp, ys, ffn_p1, ffn_s1 = ffn(1, xp, xs, True)

    return (yp, ys.reshape(Bd, 1, D),
            ckv_p[None], kpe_p[None],
            ckv_s.reshape(1, Bd, 1, KV_LORA), kpe_s.reshape(1, Bd, 1, QK_ROPE),
            conv_p[None], jnp.swapaxes(conv_s, 0, 1)[None],
            jnp.stack([ffn_p0, ffn_p1]), jnp.stack([ffn_s0, ffn_s1]))
```

```python
import functools
import math
from typing import NamedTuple

import jax
import jax.numpy as jnp
from jax import lax
from jax.experimental import pallas as pl
from jax.experimental.pallas import tpu as pltpu

D_MODEL = 2048
N_HEADS = 16
Q_LORA = 512
KV_LORA = 512
QK_NOPE = 128
QK_ROPE = 64
V_DIM = 128
ATTN_SCALE = 1.0 / math.sqrt(QK_NOPE + QK_ROPE)
Q_SCALE = ATTN_SCALE * math.log2(math.e)
ROPE_BASE = 10000.0
CONV_W = 31
D_FF = 5632
EPS = 1e-6

HEAD_W = 256
LANE = 128
SUBLANE = 8
HALO_FFN = 8
HALO_CONV = 32
V7X_VMEM_LIMIT = 56 * 1024 * 1024

BF = jnp.bfloat16
F32 = jnp.float32


class Grp(NamedTuple):
    G: int
    S: int
    per_row: bool


def _params(*sem):
    return pltpu.CompilerParams(dimension_semantics=sem, vmem_limit_bytes=V7X_VMEM_LIMIT)


def _dot(a, b):
    return jnp.dot(a, b, preferred_element_type=F32)


def _dot_nt(a, b):
    return lax.dot_general(a, b, (((1,), (1,)), ((), ())), preferred_element_type=F32)


def _rms(x, g):
    return x * lax.rsqrt(jnp.mean(x * x, axis=-1, keepdims=True) + EPS) * g


def _silu(x):
    return x * jax.nn.sigmoid(x)


def _modulated_norm(x, g, sc, sh):
    return (_rms(x, g) * (1.0 + sc) + sh).astype(BF)


def _mod_spec(grp, ts, k):
    if grp.per_row:
        return pl.BlockSpec((None, ts, D_MODEL), lambda g, t, *_: (g, t, k))
    return pl.BlockSpec((None, 1, D_MODEL), lambda g, t, *_: (g, 0, k))


def _row_spec(ts, width):
    return pl.BlockSpec((None, ts, width), lambda g, t, *_: (g, t, 0))


def _const_spec(shape):
    nd = len(shape)
    return pl.BlockSpec(shape, lambda *_: (0,) * nd)


def _mod_kernel(c_ref, w_ref, b_ref, o_ref):
    c = c_ref[...]
    o_ref[...] = _dot(_silu(c).astype(BF), w_ref[...].astype(BF)) + b_ref[...]


def _modulation(c_all, w_mod, b_mod):
    L, D, N = w_mod.shape
    R = c_all.shape[0]
    tn = 1024
    return pl.pallas_call(
        _mod_kernel,
        out_shape=jax.ShapeDtypeStruct((L, R, N), F32),
        grid=(L, N // tn),
        in_specs=[pl.BlockSpec((R, D), lambda l, j: (0, 0)),
                  pl.BlockSpec((None, D, tn), lambda l, j: (l, 0, j)),
                  pl.BlockSpec((None, 1, tn), lambda l, j: (l, 0, j))],
        out_specs=pl.BlockSpec((None, R, tn), lambda l, j: (l, 0, j)),
        compiler_params=_params("arbitrary", "arbitrary"),
        name="modulation",
    )(c_all, w_mod, b_mod.reshape(L, 1, N))


def _proj_kernel(*refs, expand_kv):
    if expand_kv:
        (x_ref, sh_ref, sc_ref, cos_ref, sin_ref, ng_ref, wa_ref, qg_ref, wqb_ref, kvg_ref,
         wuk_ref, wuv_ref, q_ref, ckv_ref, kpe_ref, k_ref, v_ref) = refs
    else:
        (x_ref, sh_ref, sc_ref, cos_ref, sin_ref, ng_ref, wa_ref, qg_ref, wqb_ref, kvg_ref,
         q_ref, ckv_ref, kpe_ref) = refs
    h = _modulated_norm(x_ref[...], ng_ref[...], sc_ref[...], sh_ref[...])
    qkv = _dot(h, wa_ref[...])
    qn = _rms(qkv[:, :Q_LORA], qg_ref[...]).astype(BF)
    ckv = _rms(qkv[:, Q_LORA:Q_LORA + KV_LORA], kvg_ref[...])
    cos = cos_ref[...]
    sin = sin_ref[...]
    o = Q_LORA + KV_LORA
    kpe = qkv[:, o:o + LANE] * cos + qkv[:, o + LANE:o + 2 * LANE] * sin
    ckv_ref[...] = ckv
    kpe_ref[...] = kpe[:, :QK_ROPE]
    qm = _dot(qn, wqb_ref[...])
    sw0 = N_HEADS * HEAD_W
    for hd in range(N_HEADS):
        a = hd * HEAD_W
        q_ref[:, a:a + LANE] = (qm[:, a:a + LANE] * Q_SCALE).astype(BF)
        pe = qm[:, a + LANE:a + 2 * LANE] * cos + qm[:, sw0 + hd * LANE:sw0 + (hd + 1) * LANE] * sin
        q_ref[:, a + LANE:a + 2 * LANE] = (pe * Q_SCALE).astype(BF)
    if expand_kv:
        cb = ckv.astype(BF)
        kn = _dot(cb, wuk_ref[...])
        kpe_b = kpe.astype(BF)
        for hd in range(N_HEADS):
            a = hd * HEAD_W
            k_ref[:, a:a + LANE] = kn[:, hd * LANE:(hd + 1) * LANE].astype(BF)
            k_ref[:, a + LANE:a + 2 * LANE] = kpe_b
        v_ref[...] = _dot(cb, wuv_ref[...]).astype(BF)


def _mla_project(grp, ts, x, mod, cos, sin, norm_g, w_a, q_norm_g, wqb, kv_norm_g, w_uk=None, w_uv=None):
    expand_kv = w_uk is not None
    G, S = grp.G, grp.S
    qw = N_HEADS * HEAD_W
    in_specs = [_row_spec(ts, D_MODEL), _mod_spec(grp, ts, 0), _mod_spec(grp, ts, 1),
                pl.BlockSpec((ts, LANE), lambda g, t: (t, 0)), pl.BlockSpec((ts, LANE), lambda g, t: (t, 0)),
                _const_spec((1, D_MODEL)), _const_spec(w_a.shape), _const_spec((1, Q_LORA)),
                _const_spec(wqb.shape), _const_spec((1, KV_LORA))]
    args = [x, mod, mod, cos, sin, norm_g, w_a, q_norm_g, wqb, kv_norm_g]
    out_shape = [jax.ShapeDtypeStruct((G, S, qw), BF), jax.ShapeDtypeStruct((G, S, KV_LORA), F32),
                 jax.ShapeDtypeStruct((G, S, QK_ROPE), F32)]
    out_specs = [_row_spec(ts, qw), _row_spec(ts, KV_LORA), _row_spec(ts, QK_ROPE)]
    if expand_kv:
        in_specs += [_const_spec(w_uk.shape), _const_spec(w_uv.shape)]
        args += [w_uk, w_uv]
        out_shape += [jax.ShapeDtypeStruct((G, S, qw), BF), jax.ShapeDtypeStruct((G, S, N_HEADS * V_DIM), BF)]
        out_specs += [_row_spec(ts, qw), _row_spec(ts, N_HEADS * V_DIM)]
    return pl.pallas_call(
        functools.partial(_proj_kernel, expand_kv=expand_kv),
        out_shape=out_shape, grid=(G, S // ts), in_specs=in_specs, out_specs=out_specs,
        compiler_params=_params("arbitrary", "arbitrary"),
        name="mla_project_kv" if expand_kv else "mla_project",
    )(*args)


def _attn_kernel(q_ref, k_ref, v_ref, o_ref, *, tq, seq):
    row = lax.broadcasted_iota(jnp.int32, (tq, tq), 0)
    col = lax.broadcasted_iota(jnp.int32, (tq, tq), 1)
    causal = col <= row
    for i in range(seq // tq):
        lo, hi = i * tq, (i + 1) * tq
        q = q_ref[lo:hi, :]
        sd = jnp.where(causal, _dot_nt(q, k_ref[lo:hi, :]), -jnp.inf)
        m = jnp.max(sd, axis=-1, keepdims=True)
        if i > 0:
            so = _dot_nt(q, k_ref[0:lo, :])
            m = jnp.maximum(m, jnp.max(so, axis=-1, keepdims=True))
        pd = jnp.exp2(sd - m)
        l = jnp.sum(pd, axis=-1, keepdims=True)
        acc = _dot(pd.astype(BF), v_ref[lo:hi, :])
        if i > 0:
            po = jnp.exp2(so - m)
            l = l + jnp.sum(po, axis=-1, keepdims=True)
            acc = acc + _dot(po.astype(BF), v_ref[0:lo, :])
        o_ref[lo:hi, :] = (acc / l).astype(o_ref.dtype)


def _prompt_attention(q, k, v, tq):
    B, S, _ = q.shape
    return pl.pallas_call(
        functools.partial(_attn_kernel, tq=tq, seq=S),
        out_shape=jax.ShapeDtypeStruct((B, S, N_HEADS * V_DIM), BF),
        grid=(B, N_HEADS),
        in_specs=[pl.BlockSpec((None, S, HEAD_W), lambda b, h: (b, 0, h)),
                  pl.BlockSpec((None, S, HEAD_W), lambda b, h: (b, 0, h)),
                  pl.BlockSpec((None, S, V_DIM), lambda b, h: (b, 0, h))],
        out_specs=pl.BlockSpec((None, S, V_DIM), lambda b, h: (b, 0, h)),
        compiler_params=_params("arbitrary", "arbitrary"),
        name="prompt_attention",
    )(q, k, v)


def _absorb_kernel(q_ref, wukt_ref, qlat_ref, qpe_ref):
    q = q_ref[...]
    qlat_ref[...] = _dot(q[:, :QK_NOPE], wukt_ref[...]).astype(BF)
    qpe_ref[...] = q[:, QK_NOPE:]


def _absorb_q(q, w_ukt):
    Bd = q.shape[0]
    return pl.pallas_call(
        _absorb_kernel,
        out_shape=[jax.ShapeDtypeStruct((Bd, N_HEADS * KV_LORA), BF),
                   jax.ShapeDtypeStruct((Bd, N_HEADS * LANE), BF)],
        grid=(N_HEADS,),
        in_specs=[pl.BlockSpec((Bd, HEAD_W), lambda h: (0, h)),
                  pl.BlockSpec((None, QK_NOPE, KV_LORA), lambda h: (h, 0, 0))],
        out_specs=[pl.BlockSpec((Bd, KV_LORA), lambda h: (0, h)),
                   pl.BlockSpec((Bd, LANE), lambda h: (0, h))],
        compiler_params=_params("arbitrary"),
        name="absorb_q",
    )(q, w_ukt)


DEC_SLOTS = 8
DEC_AHEAD = 4


def _decode_kernel(pt_ref, qlat_ref, qpe_ref, ckvn_ref, kpen_ref, cckv_hbm, ckpet_hbm, o_ref,
                   ckv_buf, kpe_buf, kcb, sem, *, layer, n_pages, ch):
    b = pl.program_id(0)
    nb = pl.num_programs(0)
    n_ch = n_pages // ch
    page = ckv_buf.shape[2]

    def chunk_copies(bb, c):
        slot = c % DEC_SLOTS
        cps = []
        for i in range(ch):
            pg = pt_ref[bb * n_pages + c * ch + i]
            cps.append(pltpu.make_async_copy(cckv_hbm.at[layer, pg], ckv_buf.at[slot, i], sem.at[0, slot]))
            cps.append(pltpu.make_async_copy(ckpet_hbm.at[layer, pg], kpe_buf.at[slot, i], sem.at[1, slot]))
        return cps

    def start(bb, c):
        for n, cp in enumerate(chunk_copies(bb, c)):
            cp.start(priority=(n // 2) % 2)

    def start_ahead(c):
        if c < n_ch:
            start(b, c)
        else:
            @pl.when(b + 1 < nb)
            def _():
                start(b + 1, c - n_ch)

    @pl.when(b == 0)
    def _():
        for c in range(DEC_AHEAD):
            start(b, c)

    qlat = qlat_ref[...]
    qpe = qpe_ref[...][:, :QK_ROPE]
    ckvn = ckvn_ref[...].astype(BF).astype(F32)
    kpen = kpen_ref[...].astype(BF).astype(F32)

    def scores(c):
        for cp in chunk_copies(b, c):
            cp.wait()
        slot = c % DEC_SLOTS
        kc = ckv_buf[slot].reshape(ch * page, KV_LORA).astype(BF)
        kcb[c % 2] = kc
        pe = [_dot(qpe, kpe_buf[slot, i].astype(BF)) for i in range(ch)]
        return _dot_nt(qlat, kc) + jnp.concatenate(pe, axis=1)

    m = (jnp.sum(qlat.astype(F32) * ckvn, axis=-1, keepdims=True)
         + jnp.sum(qpe.astype(F32) * kpen, axis=-1, keepdims=True))
    l = jnp.ones_like(m)
    acc = jnp.broadcast_to(ckvn, (N_HEADS, KV_LORA))

    s = scores(0)
    for c in range(n_ch):
        start_ahead(c + DEC_AHEAD)
        s_next = scores(c + 1) if c + 1 < n_ch else None
        m_new = jnp.maximum(m, jnp.max(s, axis=-1, keepdims=True))
        alpha = jnp.exp2(m - m_new)
        p = jnp.exp2(s - m_new)
        l = alpha * l + jnp.sum(p, axis=-1, keepdims=True)
        acc = alpha * acc + _dot(p.astype(BF), kcb[c % 2])
        m = m_new
        s = s_next
    o_ref[...] = (acc / l).astype(o_ref.dtype)


def _paged_decode(page_table, qlat, qpe, ckv_new, kpe_new, cache_ckv, cache_kpet, layer, ch=8):
    Bd, n_pages = page_table.shape
    page = cache_ckv.shape[2]
    n_ch = n_pages // ch
    assert n_pages % ch == 0 and n_ch % DEC_SLOTS == 0 and n_ch >= DEC_AHEAD
    grid_spec = pltpu.PrefetchScalarGridSpec(
        num_scalar_prefetch=1,
        grid=(Bd,),
        in_specs=[pl.BlockSpec((None, N_HEADS, KV_LORA), lambda b, pt: (b, 0, 0)),
                  pl.BlockSpec((None, N_HEADS, LANE), lambda b, pt: (b, 0, 0)),
                  pl.BlockSpec((None, 1, KV_LORA), lambda b, pt: (b, 0, 0)),
                  pl.BlockSpec((None, 1, QK_ROPE), lambda b, pt: (b, 0, 0)),
                  pl.BlockSpec(memory_space=pl.ANY),
                  pl.BlockSpec(memory_space=pl.ANY)],
        out_specs=pl.BlockSpec((None, N_HEADS, KV_LORA), lambda b, pt: (b, 0, 0)),
        scratch_shapes=[pltpu.VMEM((DEC_SLOTS, ch, page, KV_LORA), F32),
                        pltpu.VMEM((DEC_SLOTS, ch, QK_ROPE, page), F32),
                        pltpu.VMEM((2, ch * page, KV_LORA), BF),
                        pltpu.SemaphoreType.DMA((2, DEC_SLOTS))],
    )
    return pl.pallas_call(
        functools.partial(_decode_kernel, layer=layer, n_pages=n_pages, ch=ch),
        out_shape=jax.ShapeDtypeStruct((Bd, N_HEADS, KV_LORA), BF),
        grid_spec=grid_spec,
        compiler_params=_params("arbitrary"),
        name="paged_decode",
    )(page_table.reshape(-1), qlat, qpe, ckv_new, kpe_new, cache_ckv, cache_kpet)


def _unabsorb_kernel(olat_ref, wuv_ref, o_ref):
    o_ref[...] = _dot(olat_ref[...], wuv_ref[...]).astype(o_ref.dtype)


def _unabsorb_o(olat, w_uv):
    Bd = olat.shape[0]
    return pl.pallas_call(
        _unabsorb_kernel,
        out_shape=jax.ShapeDtypeStruct((Bd, N_HEADS * V_DIM), BF),
        grid=(N_HEADS,),
        in_specs=[pl.BlockSpec((Bd, KV_LORA), lambda h: (0, h)),
                  pl.BlockSpec((KV_LORA, V_DIM), lambda h: (0, h))],
        out_specs=pl.BlockSpec((Bd, V_DIM), lambda h: (0, h)),
        compiler_params=_params("arbitrary"),
        name="unabsorb_o",
    )(olat, w_uv)


def _out_kernel(*refs, has_bias):
    if has_bias:
        a_ref, w_ref, b_ref, x_ref, gt_ref, o_ref = refs
    else:
        a_ref, w_ref, x_ref, gt_ref, o_ref = refs
    y = _dot(a_ref[...], w_ref[...])
    if has_bias:
        y = y + b_ref[...]
    o_ref[...] = x_ref[...] + gt_ref[...] * y


def _out_residual(grp, ts, a, w, bias, x, mod, gate_k, name):
    G, S = grp.G, grp.S
    K = a.shape[-1]
    has_bias = bias is not None
    in_specs = [_row_spec(ts, K), _const_spec(w.shape)]
    args = [a, w]
    if has_bias:
        in_specs.append(_const_spec((1, D_MODEL)))
        args.append(bias)
    in_specs += [_row_spec(ts, D_MODEL), _mod_spec(grp, ts, gate_k)]
    args += [x, mod]
    return pl.pallas_call(
        functools.partial(_out_kernel, has_bias=has_bias),
        out_shape=jax.ShapeDtypeStruct((G, S, D_MODEL), F32),
        grid=(G, S // ts), in_specs=in_specs, out_specs=_row_spec(ts, D_MODEL),
        compiler_params=_params("arbitrary", "arbitrary"),
        name=name,
    )(*args)


def _ffn_kernel(*refs, sample, final_norm, ts):
    if sample:
        (x_ref, s0_ref, s1_ref, sh_ref, sc_ref, gt_ref, ng_ref, wg_ref, wu_ref, wc_ref, bc_ref, wd_ref, fg_ref,
         o_ref, ns0_ref, ns1_ref, wgb_ref, wub_ref, wdb_ref, h_sc) = refs
    else:
        (x_ref, xh_ref, sh_ref, sc_ref, gt_ref, ng_ref, wg_ref, wu_ref, wc_ref, bc_ref, wd_ref, fg_ref,
         o_ref, gs_ref, h_sc, hh_sc) = refs
    t = pl.program_id(1)
    j = pl.program_id(2)
    nj = pl.num_programs(2)

    @pl.when(j == 0)
    def _():
        h_sc[...] = _modulated_norm(x_ref[...], ng_ref[...], sc_ref[...], sh_ref[...])
        if not sample:
            hh_sc[...] = _modulated_norm(xh_ref[...], ng_ref[...], sc_ref[...], sh_ref[...])
        o_ref[...] = jnp.zeros(o_ref.shape, F32)

    if sample:
        wg = wg_ref[...].astype(BF)
        wu = wu_ref[...].astype(BF)
        wd = wd_ref[...].astype(BF)
        wgb_ref[...] = wg
        wub_ref[...] = wu
        wdb_ref[...] = wd
    else:
        wg, wu, wd = wg_ref[...], wu_ref[...], wd_ref[...]
    h = h_sc[...]
    g = _dot(h, wg)
    u = _dot(h, wu)
    wc = wc_ref[...]
    if sample:
        s1 = s1_ref[...]
        gc = wc[0:1] * s0_ref[...] + wc[1:2] * s1 + wc[2:3] * g + bc_ref[...]
        ns0_ref[...] = s1
        ns1_ref[...] = g
    else:
        gh = jnp.where(t > 0, _dot(hh_sc[...], wg), 0.0)
        gs_ref[...] = g[ts - 2:ts, :]
        row = lax.broadcasted_iota(jnp.int32, g.shape, 0)
        p1 = gh[HALO_FFN - 1:HALO_FFN]
        p2 = gh[HALO_FFN - 2:HALO_FFN - 1]
        g1 = jnp.where(row == 0, p1, pltpu.roll(g, 1, 0))
        g2 = jnp.where(row == 0, p2, jnp.where(row == 1, p1, pltpu.roll(g, 2, 0)))
        gc = wc[0:1] * g2 + wc[1:2] * g1 + wc[2:3] * g + bc_ref[...]
    a = (_silu(gc) * u).astype(BF)
    o_ref[...] += _dot(a, wd)

    @pl.when(j == nj - 1)
    def _():
        y = x_ref[...] + gt_ref[...] * o_ref[...]
        if final_norm:
            y = _rms(y, fg_ref[...])
        o_ref[...] = y


def _conv_ffn(grp, ts, tf, layer, x, mod, norm_g, w_gate, w_up, w_conv, b_conv, w_down, final_g, final_norm,
              state=None):
    G, S = grp.G, grp.S
    sample = state is not None
    nf = D_FF // tf
    nt = S // ts
    args = [x]
    if sample:
        in_specs = [_row_spec(ts, D_MODEL),
                    pl.BlockSpec((ts, tf), lambda g, t, j: (t, j)),
                    pl.BlockSpec((ts, tf), lambda g, t, j: (t, nf + j))]
        args += [state, state]
        w_specs = [pl.BlockSpec((None, D_MODEL, tf), lambda g, t, j: (layer, 0, j)),
                   pl.BlockSpec((None, D_MODEL, tf), lambda g, t, j: (layer, 0, j)),
                   pl.BlockSpec((None, tf, D_MODEL), lambda g, t, j: (layer, j, 0))]
    else:
        hb = ts // HALO_FFN
        in_specs = [pl.BlockSpec((None, ts, D_MODEL), lambda g, t, j: (g, t, 0), pipeline_mode=pl.Buffered(1)),
                    pl.BlockSpec((None, HALO_FFN, D_MODEL), lambda g, t, j: (g, jnp.maximum(t * hb - 1, 0), 0))]
        args.append(x)
        w_specs = [pl.BlockSpec((D_MODEL, tf), lambda g, t, j: (0, j)),
                   pl.BlockSpec((D_MODEL, tf), lambda g, t, j: (0, j)),
                   pl.BlockSpec((tf, D_MODEL), lambda g, t, j: (j, 0))]
    in_specs += [_mod_spec(grp, ts, 3), _mod_spec(grp, ts, 4), _mod_spec(grp, ts, 5),
                 _const_spec((1, D_MODEL)), w_specs[0], w_specs[1],
                 pl.BlockSpec((None, 3, tf), lambda g, t, j: (layer, 0, j)),
                 pl.BlockSpec((None, 1, tf), lambda g, t, j: (layer, 0, j)),
                 w_specs[2], _const_spec((1, D_MODEL))]
    args += [mod, mod, mod, norm_g, w_gate, w_up, w_conv, b_conv, w_down, final_g]
    out_shape = [jax.ShapeDtypeStruct((G, S, D_MODEL), F32)]
    out_specs = [_row_spec(ts, D_MODEL)]
    scratch = [pltpu.VMEM((ts, D_MODEL), BF)]
    if sample:
        assert G == 1 and nt == 1
        out_shape += [jax.ShapeDtypeStruct((S, D_FF), F32)] * 2
        out_specs += [pl.BlockSpec((ts, tf), lambda g, t, j: (t, j))] * 2
        out_shape += [jax.ShapeDtypeStruct((D_MODEL, D_FF), BF)] * 2 + [jax.ShapeDtypeStruct((D_FF, D_MODEL), BF)]
        out_specs += [pl.BlockSpec((D_MODEL, tf), lambda g, t, j: (0, j)),
                      pl.BlockSpec((D_MODEL, tf), lambda g, t, j: (0, j)),
                      pl.BlockSpec((tf, D_MODEL), lambda g, t, j: (j, 0))]
    else:
        out_shape.append(jax.ShapeDtypeStruct((G, nt, 2, D_FF), F32))
        out_specs.append(pl.BlockSpec((None, None, 2, tf), lambda g, t, j: (g, t, 0, j)))
        scratch.append(pltpu.VMEM((HALO_FFN, D_MODEL), BF))
    outs = pl.pallas_call(
        functools.partial(_ffn_kernel, sample=sample, final_norm=final_norm, ts=ts),
        out_shape=out_shape, grid=(G, nt, nf), in_specs=in_specs, out_specs=out_specs,
        scratch_shapes=scratch,
        compiler_params=_params("arbitrary", "arbitrary", "arbitrary"),
        name="conv_ffn_sample" if sample else "conv_ffn",
    )(*args)
    if sample:
        return outs[0], jnp.stack([outs[1], outs[2]], axis=1), outs[3], outs[4], outs[5]
    return outs[0], outs[1][:, -1]


def _pw1_kernel(x_ref, sh_ref, sc_ref, ng_ref, wa_ref, wb_ref, ba_ref, bb_ref, u_ref, h_sc):
    @pl.when(pl.program_id(2) == 0)
    def _():
        h_sc[...] = _modulated_norm(x_ref[...], ng_ref[...], sc_ref[...], sh_ref[...])

    h = h_sc[...]
    a = _dot(h, wa_ref[...]) + ba_ref[...]
    b = _dot(h, wb_ref[...]) + bb_ref[...]
    u_ref[...] = a * jax.nn.sigmoid(b)


def _pw1_glu(grp, ts, tn, x, mod, norm_g, w_pw1, b_pw1):
    G, S = grp.G, grp.S
    nn = D_MODEL // tn
    return pl.pallas_call(
        _pw1_kernel,
        out_shape=jax.ShapeDtypeStruct((G, S, D_MODEL), F32),
        grid=(G, S // ts, nn),
        in_specs=[_row_spec(ts, D_MODEL), _mod_spec(grp, ts, 0), _mod_spec(grp, ts, 1),
                  _const_spec((1, D_MODEL)),
                  pl.BlockSpec((D_MODEL, tn), lambda g, t, j: (0, j)),
                  pl.BlockSpec((D_MODEL, tn), lambda g, t, j: (0, nn + j)),
                  pl.BlockSpec((1, tn), lambda g, t, j: (0, j)),
                  pl.BlockSpec((1, tn), lambda g, t, j: (0, nn + j))],
        out_specs=pl.BlockSpec((None, ts, tn), lambda g, t, j: (g, t, j)),
        scratch_shapes=[pltpu.VMEM((ts, D_MODEL), BF)],
        compiler_params=_params("arbitrary", "arbitrary", "arbitrary"),
        name="conv_pw1_glu",
    )(x, mod, mod, norm_g, w_pw1, w_pw1, b_pw1, b_pw1)


def _ln_swish(y, g, b):
    mu = jnp.mean(y, axis=-1, keepdims=True)
    yc = y - mu
    var = jnp.mean(yc * yc, axis=-1, keepdims=True)
    return _silu(yc * lax.rsqrt(var + EPS) * g + b).astype(BF)


def _dwconv_kernel(u_ref, uh_ref, w_ref, b_ref, lg_ref, lb_ref, y_ref, ext_sc, sh_sc, acc_sc, *, ts):
    t = pl.program_id(1)
    ext_sc[0:HALO_CONV, :] = jnp.where(t > 0, uh_ref[...], 0.0)
    ext_sc[HALO_CONV:, :] = u_ref[...]
    rows = ts + HALO_CONV
    rc = 64
    lead = HALO_CONV - (CONV_W - 1)

    for c in range(D_MODEL // LANE):
        cols = slice(c * LANE, (c + 1) * LANE)
        x = ext_sc[:, cols]
        for r in range(1, SUBLANE):
            sh_sc[r] = pltpu.roll(x, rows - r, 0)

        def body(i, carry):
            r0 = pl.multiple_of(i * rc, rc)
            acc = jnp.zeros((rc, LANE), F32) + b_ref[:, cols]
            for k in range(CONV_W):
                a, r = divmod(lead + k, SUBLANE)
                if r == 0:
                    win = ext_sc[pl.ds(r0 + SUBLANE * a, rc), cols]
                else:
                    win = sh_sc[r, pl.ds(r0 + SUBLANE * a, rc), :]
                acc = acc + w_ref[k:k + 1, cols] * win
            acc_sc[pl.ds(r0, rc), cols] = acc
            return carry

        lax.fori_loop(0, ts // rc, body, 0)
    y_ref[...] = _ln_swish(acc_sc[...], lg_ref[...], lb_ref[...])


def _dwconv_prompt(ts, u, w_dw, b_dw, ln_g, ln_b):
    G, S, _ = u.shape
    hb = ts // HALO_CONV
    return pl.pallas_call(
        functools.partial(_dwconv_kernel, ts=ts),
        out_shape=jax.ShapeDtypeStruct((G, S, D_MODEL), BF),
        grid=(G, S // ts),
        in_specs=[_row_spec(ts, D_MODEL),
                  pl.BlockSpec((None, HALO_CONV, D_MODEL), lambda g, t: (g, jnp.maximum(t * hb - 1, 0), 0)),
                  _const_spec((CONV_W, D_MODEL)), _const_spec((1, D_MODEL)),
                  _const_spec((1, D_MODEL)), _const_spec((1, D_MODEL))],
        out_specs=_row_spec(ts, D_MODEL),
        scratch_shapes=[pltpu.VMEM((ts + HALO_CONV, D_MODEL), F32),
                        pltpu.VMEM((SUBLANE, ts + HALO_CONV, LANE), F32),
                        pltpu.VMEM((ts, D_MODEL), F32)],
        compiler_params=_params("arbitrary", "arbitrary"),
        name="dwconv_ln_swish",
    )(u, u, w_dw, b_dw, ln_g, ln_b)


def _dwconv_s_kernel(st_ref, u_ref, w_ref, b_ref, lg_ref, lb_ref, y_ref, ns_ref):
    nprev = CONV_W - 1
    u = u_ref[...]
    acc = b_ref[...] + w_ref[nprev:nprev + 1, :] * u
    for k in range(nprev):
        acc = acc + w_ref[k:k + 1, :] * st_ref[k]
    for k in range(nprev - 1):
        ns_ref[k] = st_ref[k + 1]
    ns_ref[nprev - 1] = u
    y_ref[...] = _ln_swish(acc, lg_ref[...], lb_ref[...])


def _dwconv_sample(rs, state_t, u, w_dw, b_dw, ln_g, ln_b):
    nprev, Bd, _ = state_t.shape
    st_spec = pl.BlockSpec((nprev, rs, D_MODEL), lambda i: (0, i, 0))
    return pl.pallas_call(
        _dwconv_s_kernel,
        out_shape=[jax.ShapeDtypeStruct((Bd, D_MODEL), BF), jax.ShapeDtypeStruct(state_t.shape, F32)],
        grid=(Bd // rs,),
        in_specs=[st_spec, pl.BlockSpec((rs, D_MODEL), lambda i: (i, 0)),
                  _const_spec((CONV_W, D_MODEL)), _const_spec((1, D_MODEL)),
                  _const_spec((1, D_MODEL)), _const_spec((1, D_MODEL))],
        out_specs=[pl.BlockSpec((rs, D_MODEL), lambda i: (i, 0)), st_spec],
        compiler_params=_params("arbitrary"),
        name="dwconv_ln_swish_sample",
    )(state_t, u, w_dw, b_dw, ln_g, ln_b)


def _rope_tables(pos):
    half = QK_ROPE // 2
    inv = ROPE_BASE ** (-jnp.arange(half, dtype=F32) / half)
    ang = pos.astype(F32)[:, None] * inv[None, :]
    cos, sin = jnp.cos(ang), jnp.sin(ang)
    z = jnp.zeros((pos.shape[0], LANE - QK_ROPE), F32)
    return jnp.concatenate([cos, cos, z], axis=1), jnp.concatenate([-sin, sin, z], axis=1)


def _swap_halves(w):
    half = QK_ROPE // 2
    return jnp.concatenate([w[..., half:], w[..., :half]], axis=-1)


def _mla_weights(wq_a, wq_b, wkv_a):
    z = jnp.zeros((D_MODEL, LANE - QK_ROPE), F32)
    pe = wkv_a[:, KV_LORA:]
    w_a = jnp.concatenate([wq_a, wkv_a[:, :KV_LORA], pe, z, _swap_halves(pe), z], axis=1).astype(BF)
    wqb = wq_b.reshape(Q_LORA, N_HEADS, QK_NOPE + QK_ROPE)
    nope, qpe = wqb[..., :QK_NOPE], wqb[..., QK_NOPE:]
    zq = jnp.zeros((Q_LORA, N_HEADS, LANE - QK_ROPE), F32)
    main = jnp.concatenate([nope, qpe, zq], axis=-1).reshape(Q_LORA, N_HEADS * HEAD_W)
    swapped = jnp.concatenate([_swap_halves(qpe), zq], axis=-1).reshape(Q_LORA, N_HEADS * LANE)
    return w_a, jnp.concatenate([main, swapped], axis=1).astype(BF)


def kernel(x_prompt, x_sample, cache_ckv, cache_kpe, state_conv, state_ffn, page_table, c_prompt, c_sample,
           w_mod, b_mod, norm_mix_g, norm_ffn_g, wq_a, q_norm_g, wq_b, wkv_a, kv_norm_g, w_uk, w_uv, w_o,
           conv_w_pw1, conv_b_pw1, conv_w_dw, conv_b_dw, conv_ln_g, conv_ln_b, conv_w_pw2, conv_b_pw2,
           ffn_w_gate, ffn_w_up, ffn_w_conv, ffn_b_conv, ffn_w_down, final_norm_g):
    D = D_MODEL
    B, S, _ = x_prompt.shape
    Bd = x_sample.shape[0]
    past_len = page_table.shape[1] * cache_ckv.shape[2]
    grp_p = Grp(B, S, False)
    grp_s = Grp(1, Bd, True)

    pad = (-(Bd + B)) % 8
    c_all = jnp.concatenate([c_sample, c_prompt, jnp.zeros((pad, D), F32)], axis=0)
    mod = _modulation(c_all, w_mod, b_mod)
    mod_s = [mod[i, :Bd][None] for i in range(2)]
    mod_p = [mod[i, Bd:Bd + B][:, None] for i in range(2)]

    row = lambda v: v.reshape(1, -1)
    xp = x_prompt
    xs = x_sample.reshape(1, Bd, D)

    w_a, wqb = _mla_weights(wq_a[0], wq_b[0], wkv_a[0])
    w_uk_flat = w_uk[0].reshape(KV_LORA, N_HEADS * QK_NOPE).astype(BF)
    w_uv_flat = w_uv[0].reshape(KV_LORA, N_HEADS * V_DIM).astype(BF)
    w_ukt = jnp.transpose(w_uk[0], (1, 2, 0)).astype(BF)
    w_o_b = w_o[0].astype(BF)
    cos_p, sin_p = _rope_tables(jnp.arange(S, dtype=jnp.int32))
    cos_s, sin_s = _rope_tables(jnp.full((Bd,), past_len, dtype=jnp.int32))
    ng, qg, kvg = row(norm_mix_g[0]), row(q_norm_g[0]), row(kv_norm_g[0])

    q_p, ckv_p, kpe_p, k_p, v_p = _mla_project(grp_p, 256, xp, mod_p[0], cos_p, sin_p, ng, w_a, qg, wqb, kvg,
                                               w_uk_flat, w_uv_flat)
    o_p = _prompt_attention(q_p, k_p, v_p, 256)
    xp = _out_residual(grp_p, 512, o_p, w_o_b, None, xp, mod_p[0], 2, "attn_out_residual")

    q_s, ckv_s, kpe_s = _mla_project(grp_s, Bd, xs, mod_s[0], cos_s, sin_s, ng, w_a, qg, wqb, kvg)
    qlat, qpe = _absorb_q(q_s[0], w_ukt)
    cache_kpet = jnp.swapaxes(cache_kpe, 2, 3)
    olat = _paged_decode(page_table, qlat.reshape(Bd, N_HEADS, KV_LORA), qpe.reshape(Bd, N_HEADS, LANE),
                         ckv_s.reshape(Bd, 1, KV_LORA), kpe_s.reshape(Bd, 1, QK_ROPE), cache_ckv, cache_kpet, 0)
    o_s = _unabsorb_o(olat.reshape(Bd, N_HEADS * KV_LORA), w_uv_flat)
    xs = _out_residual(grp_s, Bd, o_s[None], w_o_b, None, xs, mod_s[0], 2, "attn_out_residual_sample")

    fg = row(final_norm_g)
    bc = ffn_b_conv[:, None, :]

    def ffn(i, xp, xs, final_norm):
        ngf = row(norm_ffn_g[i])
        xs, st_s, wg, wu, wd = _conv_ffn(grp_s, Bd, 512, i, xs, mod_s[i], ngf, ffn_w_gate, ffn_w_up, ffn_w_conv, bc,
                                         ffn_w_down, fg, final_norm, state=state_ffn[i].reshape(Bd, 2 * D_FF))
        xp, st_p = _conv_ffn(grp_p, 1024, 512, i, xp, mod_p[i], ngf, wg, wu, ffn_w_conv, bc, wd, fg, final_norm)
        return xp, xs, st_p, st_s

    xp, xs, ffn_p0, ffn_s0 = ffn(0, xp, xs, False)

    w_pw1 = conv_w_pw1[0].astype(BF)
    b_pw1 = row(conv_b_pw1[0])
    w_pw2 = conv_w_pw2[0].astype(BF)
    b_pw2 = row(conv_b_pw2[0])
    ng1 = row(norm_mix_g[1])
    dw = (conv_w_dw[0], row(conv_b_dw[0]), row(conv_ln_g[0]), row(conv_ln_b[0]))

    u_p = _pw1_glu(grp_p, 1024, 512, xp, mod_p[1], ng1, w_pw1, b_pw1)
    y_p = _dwconv_prompt(256, u_p, *dw)
    xp = _out_residual(grp_p, 512, y_p, w_pw2, b_pw2, xp, mod_p[1], 2, "conv_out_residual")
    conv_p = u_p[:, S - (CONV_W - 1):]

    u_s = _pw1_glu(grp_s, Bd, 512, xs, mod_s[1], ng1, w_pw1, b_pw1)
    y_s, conv_s = _dwconv_sample(16, jnp.swapaxes(state_conv[0], 0, 1), u_s[0], *dw)
    xs = _out_residual(grp_s, Bd, y_s[None], w_pw2, b_pw2, xs, mod_s[1], 2, "conv_out_residual_sample")

    yp, ys, ffn_p1, ffn_s1 = ffn(1, xp, xs, True)

    return (yp, ys.reshape(Bd, 1, D),
            ckv_p[None], kpe_p[None],
            ckv_s.reshape(1, Bd, 1, KV_LORA), kpe_s.reshape(1, Bd, 1, QK_ROPE),
            conv_p[None], jnp.swapaxes(conv_s, 0, 1)[None],
            jnp.stack([ffn_p0, ffn_p1]), jnp.stack([ffn_s0, ffn_s1]))
```

```python
import functools
import math
from typing import NamedTuple

import jax
import jax.numpy as jnp
from jax import lax
from jax.experimental import pallas as pl
from jax.experimental.pallas import tpu as pltpu

D_MODEL = 2048
N_HEADS = 16
Q_LORA = 512
KV_LORA = 512
QK_NOPE = 128
QK_ROPE = 64
V_DIM = 128
ATTN_SCALE = 1.0 / math.sqrt(QK_NOPE + QK_ROPE)
Q_SCALE = ATTN_SCALE * math.log2(math.e)
ROPE_BASE = 10000.0
CONV_W = 31
D_FF = 5632
EPS = 1e-6

HEAD_W = 256
LANE = 128
SUBLANE = 8
HALO_FFN = 8
HALO_CONV = 32
V7X_VMEM_LIMIT = 56 * 1024 * 1024

BF = jnp.bfloat16
F32 = jnp.float32


class Grp(NamedTuple):
    G: int
    S: int
    per_row: bool


def _params(*sem):
    return pltpu.CompilerParams(dimension_semantics=sem, vmem_limit_bytes=V7X_VMEM_LIMIT)


def _dot(a, b):
    return jnp.dot(a, b, preferred_element_type=F32)


def _dot_nt(a, b):
    return lax.dot_general(a, b, (((1,), (1,)), ((), ())), preferred_element_type=F32)


def _rms(x, g):
    return x * lax.rsqrt(jnp.mean(x * x, axis=-1, keepdims=True) + EPS) * g


def _silu(x):
    return x * jax.nn.sigmoid(x)


def _modulated_norm(x, g, sc, sh):
    return (_rms(x, g) * (1.0 + sc) + sh).astype(BF)


def _mod_spec(grp, ts, k):
    if grp.per_row:
        return pl.BlockSpec((None, ts, D_MODEL), lambda g, t, *_: (g, t, k))
    return pl.BlockSpec((None, 1, D_MODEL), lambda g, t, *_: (g, 0, k))


def _row_spec(ts, width):
    return pl.BlockSpec((None, ts, width), lambda g, t, *_: (g, t, 0))


def _const_spec(shape):
    nd = len(shape)
    return pl.BlockSpec(shape, lambda *_: (0,) * nd)


def _mod_kernel(c_ref, w_ref, b_ref, o_ref):
    c = c_ref[...]
    o_ref[...] = _dot(_silu(c).astype(BF), w_ref[...].astype(BF)) + b_ref[...]


def _modulation(c_all, w_mod, b_mod):
    L, D, N = w_mod.shape
    R = c_all.shape[0]
    tn = 1024
    return pl.pallas_call(
        _mod_kernel,
        out_shape=jax.ShapeDtypeStruct((L, R, N), F32),
        grid=(L, N // tn),
        in_specs=[pl.BlockSpec((R, D), lambda l, j: (0, 0)),
                  pl.BlockSpec((None, D, tn), lambda l, j: (l, 0, j)),
                  pl.BlockSpec((None, 1, tn), lambda l, j: (l, 0, j))],
        out_specs=pl.BlockSpec((None, R, tn), lambda l, j: (l, 0, j)),
        compiler_params=_params("arbitrary", "arbitrary"),
        name="modulation",
    )(c_all, w_mod, b_mod.reshape(L, 1, N))


def _proj_kernel(*refs, expand_kv):
    if expand_kv:
        (x_ref, sh_ref, sc_ref, cos_ref, sin_ref, ng_ref, wa_ref, qg_ref, wqb_ref, kvg_ref,
         wuk_ref, wuv_ref, q_ref, ckv_ref, kpe_ref, k_ref, v_ref) = refs
    else:
        (x_ref, sh_ref, sc_ref, cos_ref, sin_ref, ng_ref, wa_ref, qg_ref, wqb_ref, kvg_ref,
         q_ref, ckv_ref, kpe_ref) = refs
    h = _modulated_norm(x_ref[...], ng_ref[...], sc_ref[...], sh_ref[...])
    qkv = _dot(h, wa_ref[...])
    qn = _rms(qkv[:, :Q_LORA], qg_ref[...]).astype(BF)
    ckv = _rms(qkv[:, Q_LORA:Q_LORA + KV_LORA], kvg_ref[...])
    cos = cos_ref[...]
    sin = sin_ref[...]
    o = Q_LORA + KV_LORA
    kpe = qkv[:, o:o + LANE] * cos + qkv[:, o + LANE:o + 2 * LANE] * sin
    ckv_ref[...] = ckv
    kpe_ref[...] = kpe[:, :QK_ROPE]
    qm = _dot(qn, wqb_ref[...])
    sw0 = N_HEADS * HEAD_W
    for hd in range(N_HEADS):
        a = hd * HEAD_W
        q_ref[:, a:a + LANE] = (qm[:, a:a + LANE] * Q_SCALE).astype(BF)
        pe = qm[:, a + LANE:a + 2 * LANE] * cos + qm[:, sw0 + hd * LANE:sw0 + (hd + 1) * LANE] * sin
        q_ref[:, a + LANE:a + 2 * LANE] = (pe * Q_SCALE).astype(BF)
    if expand_kv:
        cb = ckv.astype(BF)
        kn = _dot(cb, wuk_ref[...])
        kpe_b = kpe.astype(BF)
        for hd in range(N_HEADS):
            a = hd * HEAD_W
            k_ref[:, a:a + LANE] = kn[:, hd * LANE:(hd + 1) * LANE].astype(BF)
            k_ref[:, a + LANE:a + 2 * LANE] = kpe_b
        v_ref[...] = _dot(cb, wuv_ref[...]).astype(BF)


def _mla_project(grp, ts, x, mod, cos, sin, norm_g, w_a, q_norm_g, wqb, kv_norm_g, w_uk=None, w_uv=None):
    expand_kv = w_uk is not None
    G, S = grp.G, grp.S
    qw = N_HEADS * HEAD_W
    in_specs = [_row_spec(ts, D_MODEL), _mod_spec(grp, ts, 0), _mod_spec(grp, ts, 1),
                pl.BlockSpec((ts, LANE), lambda g, t: (t, 0)), pl.BlockSpec((ts, LANE), lambda g, t: (t, 0)),
                _const_spec((1, D_MODEL)), _const_spec(w_a.shape), _const_spec((1, Q_LORA)),
                _const_spec(wqb.shape), _const_spec((1, KV_LORA))]
    args = [x, mod, mod, cos, sin, norm_g, w_a, q_norm_g, wqb, kv_norm_g]
    out_shape = [jax.ShapeDtypeStruct((G, S, qw), BF), jax.ShapeDtypeStruct((G, S, KV_LORA), F32),
                 jax.ShapeDtypeStruct((G, S, QK_ROPE), F32)]
    out_specs = [_row_spec(ts, qw), _row_spec(ts, KV_LORA), _row_spec(ts, QK_ROPE)]
    if expand_kv:
        in_specs += [_const_spec(w_uk.shape), _const_spec(w_uv.shape)]
        args += [w_uk, w_uv]
        out_shape += [jax.ShapeDtypeStruct((G, S, qw), BF), jax.ShapeDtypeStruct((G, S, N_HEADS * V_DIM), BF)]
        out_specs += [_row_spec(ts, qw), _row_spec(ts, N_HEADS * V_DIM)]
    return pl.pallas_call(
        functools.partial(_proj_kernel, expand_kv=expand_kv),
        out_shape=out_shape, grid=(G, S // ts), in_specs=in_specs, out_specs=out_specs,
        compiler_params=_params("arbitrary", "arbitrary"),
        name="mla_project_kv" if expand_kv else "mla_project",
    )(*args)


def _attn_kernel(q_ref, k_ref, v_ref, o_ref, *, tq, seq):
    row = lax.broadcasted_iota(jnp.int32, (tq, tq), 0)
    col = lax.broadcasted_iota(jnp.int32, (tq, tq), 1)
    causal = col <= row
    for i in range(seq // tq):
        lo, hi = i * tq, (i + 1) * tq
        q = q_ref[lo:hi, :]
        sd = jnp.where(causal, _dot_nt(q, k_ref[lo:hi, :]), -jnp.inf)
        m = jnp.max(sd, axis=-1, keepdims=True)
        if i > 0:
            so = _dot_nt(q, k_ref[0:lo, :])
            m = jnp.maximum(m, jnp.max(so, axis=-1, keepdims=True))
        pd = jnp.exp2(sd - m)
        l = jnp.sum(pd, axis=-1, keepdims=True)
        acc = _dot(pd.astype(BF), v_ref[lo:hi, :])
        if i > 0:
            po = jnp.exp2(so - m)
            l = l + jnp.sum(po, axis=-1, keepdims=True)
            acc = acc + _dot(po.astype(BF), v_ref[0:lo, :])
        o_ref[lo:hi, :] = (acc / l).astype(o_ref.dtype)


def _prompt_attention(q, k, v, tq):
    B, S, _ = q.shape
    return pl.pallas_call(
        functools.partial(_attn_kernel, tq=tq, seq=S),
        out_shape=jax.ShapeDtypeStruct((B, S, N_HEADS * V_DIM), BF),
        grid=(B, N_HEADS),
        in_specs=[pl.BlockSpec((None, S, HEAD_W), lambda b, h: (b, 0, h)),
                  pl.BlockSpec((None, S, HEAD_W), lambda b, h: (b, 0, h)),
                  pl.BlockSpec((None, S, V_DIM), lambda b, h: (b, 0, h))],
        out_specs=pl.BlockSpec((None, S, V_DIM), lambda b, h: (b, 0, h)),
        compiler_params=_params("arbitrary", "arbitrary"),
        name="prompt_attention",
    )(q, k, v)


def _absorb_kernel(q_ref, wukt_ref, qlat_ref, qpe_ref):
    q = q_ref[...]
    qlat_ref[...] = _dot(q[:, :QK_NOPE], wukt_ref[...]).astype(BF)
    qpe_ref[...] = q[:, QK_NOPE:]


def _absorb_q(q, w_ukt):
    Bd = q.shape[0]
    return pl.pallas_call(
        _absorb_kernel,
        out_shape=[jax.ShapeDtypeStruct((Bd, N_HEADS * KV_LORA), BF),
                   jax.ShapeDtypeStruct((Bd, N_HEADS * LANE), BF)],
        grid=(N_HEADS,),
        in_specs=[pl.BlockSpec((Bd, HEAD_W), lambda h: (0, h)),
                  pl.BlockSpec((None, QK_NOPE, KV_LORA), lambda h: (h, 0, 0))],
        out_specs=[pl.BlockSpec((Bd, KV_LORA), lambda h: (0, h)),
                   pl.BlockSpec((Bd, LANE), lambda h: (0, h))],
        compiler_params=_params("arbitrary"),
        name="absorb_q",
    )(q, w_ukt)


DEC_SLOTS = 8
DEC_AHEAD = 4


def _decode_kernel(pt_ref, qlat_ref, qpe_ref, ckvn_ref, kpen_ref, cckv_hbm, ckpet_hbm, o_ref,
                   ckv_buf, kpe_buf, kcb, sem, *, layer, n_pages, ch):
    b = pl.program_id(0)
    nb = pl.num_programs(0)
    n_ch = n_pages // ch
    page = ckv_buf.shape[2]

    def chunk_copies(bb, c):
        slot = c % DEC_SLOTS
        cps = []
        for i in range(ch):
            pg = pt_ref[bb * n_pages + c * ch + i]
            cps.append(pltpu.make_async_copy(cckv_hbm.at[layer, pg], ckv_buf.at[slot, i], sem.at[0, slot]))
            cps.append(pltpu.make_async_copy(ckpet_hbm.at[layer, pg], kpe_buf.at[slot, i], sem.at[1, slot]))
        return cps

    def start(bb, c):
        for n, cp in enumerate(chunk_copies(bb, c)):
            cp.start(priority=(n // 2) % 2)

    def start_ahead(c):
        if c < n_ch:
            start(b, c)
        else:
            @pl.when(b + 1 < nb)
            def _():
                start(b + 1, c - n_ch)

    @pl.when(b == 0)
    def _():
        for c in range(DEC_AHEAD):
            start(b, c)

    qlat = qlat_ref[...]
    qpe = qpe_ref[...][:, :QK_ROPE]
    ckvn = ckvn_ref[...].astype(BF).astype(F32)
    kpen = kpen_ref[...].astype(BF).astype(F32)

    def scores(c):
        for cp in chunk_copies(b, c):
            cp.wait()
        slot = c % DEC_SLOTS
        kc = ckv_buf[slot].reshape(ch * page, KV_LORA).astype(BF)
        kcb[c % 2] = kc
        pe = [_dot(qpe, kpe_buf[slot, i].astype(BF)) for i in range(ch)]
        return _dot_nt(qlat, kc) + jnp.concatenate(pe, axis=1)

    m = (jnp.sum(qlat.astype(F32) * ckvn, axis=-1, keepdims=True)
         + jnp.sum(qpe.astype(F32) * kpen, axis=-1, keepdims=True))
    l = jnp.ones_like(m)
    acc = jnp.broadcast_to(ckvn, (N_HEADS, KV_LORA))

    s = scores(0)
    for c in range(n_ch):
        start_ahead(c + DEC_AHEAD)
        s_next = scores(c + 1) if c + 1 < n_ch else None
        m_new = jnp.maximum(m, jnp.max(s, axis=-1, keepdims=True))
        alpha = jnp.exp2(m - m_new)
        p = jnp.exp2(s - m_new)
        l = alpha * l + jnp.sum(p, axis=-1, keepdims=True)
        acc = alpha * acc + _dot(p.astype(BF), kcb[c % 2])
        m = m_new
        s = s_next
    o_ref[...] = (acc / l).astype(o_ref.dtype)


def _paged_decode(page_table, qlat, qpe, ckv_new, kpe_new, cache_ckv, cache_kpet, layer, ch=8):
    Bd, n_pages = page_table.shape
    page = cache_ckv.shape[2]
    n_ch = n_pages // ch
    assert n_pages % ch == 0 and n_ch % DEC_SLOTS == 0 and n_ch >= DEC_AHEAD
    grid_spec = pltpu.PrefetchScalarGridSpec(
        num_scalar_prefetch=1,
        grid=(Bd,),
        in_specs=[pl.BlockSpec((None, N_HEADS, KV_LORA), lambda b, pt: (b, 0, 0)),
                  pl.BlockSpec((None, N_HEADS, LANE), lambda b, pt: (b, 0, 0)),
                  pl.BlockSpec((None, 1, KV_LORA), lambda b, pt: (b, 0, 0)),
                  pl.BlockSpec((None, 1, QK_ROPE), lambda b, pt: (b, 0, 0)),
                  pl.BlockSpec(memory_space=pl.ANY),
                  pl.BlockSpec(memory_space=pl.ANY)],
        out_specs=pl.BlockSpec((None, N_HEADS, KV_LORA), lambda b, pt: (b, 0, 0)),
        scratch_shapes=[pltpu.VMEM((DEC_SLOTS, ch, page, KV_LORA), F32),
                        pltpu.VMEM((DEC_SLOTS, ch, QK_ROPE, page), F32),
                        pltpu.VMEM((2, ch * page, KV_LORA), BF),
                        pltpu.SemaphoreType.DMA((2, DEC_SLOTS))],
    )
    return pl.pallas_call(
        functools.partial(_decode_kernel, layer=layer, n_pages=n_pages, ch=ch),
        out_shape=jax.ShapeDtypeStruct((Bd, N_HEADS, KV_LORA), BF),
        grid_spec=grid_spec,
        compiler_params=_params("arbitrary"),
        name="paged_decode",
    )(page_table.reshape(-1), qlat, qpe, ckv_new, kpe_new, cache_ckv, cache_kpet)


def _unabsorb_kernel(olat_ref, wuv_ref, o_ref):
    o_ref[...] = _dot(olat_ref[...], wuv_ref[...]).astype(o_ref.dtype)


def _unabsorb_o(olat, w_uv):
    Bd = olat.shape[0]
    return pl.pallas_call(
        _unabsorb_kernel,
        out_shape=jax.ShapeDtypeStruct((Bd, N_HEADS * V_DIM), BF),
        grid=(N_HEADS,),
        in_specs=[pl.BlockSpec((Bd, KV_LORA), lambda h: (0, h)),
                  pl.BlockSpec((KV_LORA, V_DIM), lambda h: (0, h))],
        out_specs=pl.BlockSpec((Bd, V_DIM), lambda h: (0, h)),
        compiler_params=_params("arbitrary"),
        name="unabsorb_o",
    )(olat, w_uv)


def _out_kernel(*refs, has_bias):
    if has_bias:
        a_ref, w_ref, b_ref, x_ref, gt_ref, o_ref = refs
    else:
        a_ref, w_ref, x_ref, gt_ref, o_ref = refs
    y = _dot(a_ref[...], w_ref[...])
    if has_bias:
        y = y + b_ref[...]
    o_ref[...] = x_ref[...] + gt_ref[...] * y


def _out_residual(grp, ts, a, w, bias, x, mod, gate_k, name):
    G, S = grp.G, grp.S
    K = a.shape[-1]
    has_bias = bias is not None
    in_specs = [_row_spec(ts, K), _const_spec(w.shape)]
    args = [a, w]
    if has_bias:
        in_specs.append(_const_spec((1, D_MODEL)))
        args.append(bias)
    in_specs += [_row_spec(ts, D_MODEL), _mod_spec(grp, ts, gate_k)]
    args += [x, mod]
    return pl.pallas_call(
        functools.partial(_out_kernel, has_bias=has_bias),
        out_shape=jax.ShapeDtypeStruct((G, S, D_MODEL), F32),
        grid=(G, S // ts), in_specs=in_specs, out_specs=_row_spec(ts, D_MODEL),
        compiler_params=_params("arbitrary", "arbitrary"),
        name=name,
    )(*args)


def _ffn_kernel(*refs, sample, final_norm, ts):
    if sample:
        (x_ref, s0_ref, s1_ref, sh_ref, sc_ref, gt_ref, ng_ref, wg_ref, wu_ref, wc_ref, bc_ref, wd_ref, fg_ref,
         o_ref, ns0_ref, ns1_ref, wgb_ref, wub_ref, wdb_ref, h_sc) = refs
    else:
        (x_ref, xh_ref, sh_ref, sc_ref, gt_ref, ng_ref, wg_ref, wu_ref, wc_ref, bc_ref, wd_ref, fg_ref,
         o_ref, gs_ref, h_sc, hh_sc) = refs
    t = pl.program_id(1)
    j = pl.program_id(2)
    nj = pl.num_programs(2)

    def step(first, last):
        if first:
            h = _modulated_norm(x_ref[...], ng_ref[...], sc_ref[...], sh_ref[...])
            h_sc[...] = h
            if not sample:
                hh = _modulated_norm(xh_ref[...], ng_ref[...], sc_ref[...], sh_ref[...])
                hh_sc[...] = hh
        else:
            h = h_sc[...]
            if not sample:
                hh = hh_sc[...]
        if sample:
            wg = wg_ref[...].astype(BF)
            wu = wu_ref[...].astype(BF)
            wd = wd_ref[...].astype(BF)
            wgb_ref[...] = wg
            wub_ref[...] = wu
            wdb_ref[...] = wd
        else:
            wg, wu, wd = wg_ref[...], wu_ref[...], wd_ref[...]
        g = _dot(h, wg)
        u = _dot(h, wu)
        wc = wc_ref[...]
        if sample:
            s1 = s1_ref[...]
            gc = wc[0:1] * s0_ref[...] + wc[1:2] * s1 + wc[2:3] * g + bc_ref[...]
            ns0_ref[...] = s1
            ns1_ref[...] = g
        else:
            gh = jnp.where(t > 0, _dot(hh, wg), 0.0)
            gs_ref[...] = g[ts - 2:ts, :]
            row = lax.broadcasted_iota(jnp.int32, g.shape, 0)
            p1 = gh[HALO_FFN - 1:HALO_FFN]
            p2 = gh[HALO_FFN - 2:HALO_FFN - 1]
            g1 = jnp.where(row == 0, p1, pltpu.roll(g, 1, 0))
            g2 = jnp.where(row == 0, p2, jnp.where(row == 1, p1, pltpu.roll(g, 2, 0)))
            gc = wc[0:1] * g2 + wc[1:2] * g1 + wc[2:3] * g + bc_ref[...]
        a = (_silu(gc) * u).astype(BF)
        acc = _dot(a, wd)
        if not first:
            acc = o_ref[...] + acc
        if last:
            acc = x_ref[...] + gt_ref[...] * acc
            if final_norm:
                acc = _rms(acc, fg_ref[...])
        o_ref[...] = acc

    @pl.when(j == 0)
    def _():
        step(True, False)

    @pl.when(jnp.logical_and(j > 0, j < nj - 1))
    def _():
        step(False, False)

    @pl.when(j == nj - 1)
    def _():
        step(False, True)


def _conv_ffn(grp, ts, tf, layer, x, mod, norm_g, w_gate, w_up, w_conv, b_conv, w_down, final_g, final_norm,
              state=None):
    G, S = grp.G, grp.S
    sample = state is not None
    nf = D_FF // tf
    nt = S // ts
    assert nf >= 2
    args = [x]
    if sample:
        in_specs = [_row_spec(ts, D_MODEL),
                    pl.BlockSpec((ts, tf), lambda g, t, j: (t, j)),
                    pl.BlockSpec((ts, tf), lambda g, t, j: (t, nf + j))]
        args += [state, state]
        w_specs = [pl.BlockSpec((None, D_MODEL, tf), lambda g, t, j: (layer, 0, j)),
                   pl.BlockSpec((None, D_MODEL, tf), lambda g, t, j: (layer, 0, j)),
                   pl.BlockSpec((None, tf, D_MODEL), lambda g, t, j: (layer, j, 0))]
    else:
        hb = ts // HALO_FFN
        in_specs = [pl.BlockSpec((None, ts, D_MODEL), lambda g, t, j: (g, t, 0), pipeline_mode=pl.Buffered(1)),
                    pl.BlockSpec((None, HALO_FFN, D_MODEL), lambda g, t, j: (g, jnp.maximum(t * hb - 1, 0), 0))]
        args.append(x)
        w_specs = [pl.BlockSpec((D_MODEL, tf), lambda g, t, j: (0, j)),
                   pl.BlockSpec((D_MODEL, tf), lambda g, t, j: (0, j)),
                   pl.BlockSpec((tf, D_MODEL), lambda g, t, j: (j, 0))]
    in_specs += [_mod_spec(grp, ts, 3), _mod_spec(grp, ts, 4), _mod_spec(grp, ts, 5),
                 _const_spec((1, D_MODEL)), w_specs[0], w_specs[1],
                 pl.BlockSpec((None, 3, tf), lambda g, t, j: (layer, 0, j)),
                 pl.BlockSpec((None, 1, tf), lambda g, t, j: (layer, 0, j)),
                 w_specs[2], _const_spec((1, D_MODEL))]
    args += [mod, mod, mod, norm_g, w_gate, w_up, w_conv, b_conv, w_down, final_g]
    out_shape = [jax.ShapeDtypeStruct((G, S, D_MODEL), F32)]
    out_specs = [_row_spec(ts, D_MODEL)]
    scratch = [pltpu.VMEM((ts, D_MODEL), BF)]
    if sample:
        assert G == 1 and nt == 1
        out_shape += [jax.ShapeDtypeStruct((S, D_FF), F32)] * 2
        out_specs += [pl.BlockSpec((ts, tf), lambda g, t, j: (t, j))] * 2
        out_shape += [jax.ShapeDtypeStruct((D_MODEL, D_FF), BF)] * 2 + [jax.ShapeDtypeStruct((D_FF, D_MODEL), BF)]
        out_specs += [pl.BlockSpec((D_MODEL, tf), lambda g, t, j: (0, j)),
                      pl.BlockSpec((D_MODEL, tf), lambda g, t, j: (0, j)),
                      pl.BlockSpec((tf, D_MODEL), lambda g, t, j: (j, 0))]
    else:
        out_shape.append(jax.ShapeDtypeStruct((G, nt, 2, D_FF), F32))
        out_specs.append(pl.BlockSpec((None, None, 2, tf), lambda g, t, j: (g, t, 0, j)))
        scratch.append(pltpu.VMEM((HALO_FFN, D_MODEL), BF))
    outs = pl.pallas_call(
        functools.partial(_ffn_kernel, sample=sample, final_norm=final_norm, ts=ts),
        out_shape=out_shape, grid=(G, nt, nf), in_specs=in_specs, out_specs=out_specs,
        scratch_shapes=scratch,
        compiler_params=_params("arbitrary", "arbitrary", "arbitrary"),
        name="conv_ffn_sample" if sample else "conv_ffn",
    )(*args)
    if sample:
        return outs[0], jnp.stack([outs[1], outs[2]], axis=1), outs[3], outs[4], outs[5]
    return outs[0], outs[1][:, -1]


def _pw1_kernel(x_ref, sh_ref, sc_ref, ng_ref, wa_ref, wb_ref, ba_ref, bb_ref, u_ref, h_sc):
    def step(h):
        a = _dot(h, wa_ref[...]) + ba_ref[...]
        b = _dot(h, wb_ref[...]) + bb_ref[...]
        u_ref[...] = a * jax.nn.sigmoid(b)

    @pl.when(pl.program_id(2) == 0)
    def _():
        h = _modulated_norm(x_ref[...], ng_ref[...], sc_ref[...], sh_ref[...])
        h_sc[...] = h
        step(h)

    @pl.when(pl.program_id(2) > 0)
    def _():
        step(h_sc[...])


def _pw1_glu(grp, ts, tn, x, mod, norm_g, w_pw1, b_pw1):
    G, S = grp.G, grp.S
    nn = D_MODEL // tn
    return pl.pallas_call(
        _pw1_kernel,
        out_shape=jax.ShapeDtypeStruct((G, S, D_MODEL), F32),
        grid=(G, S // ts, nn),
        in_specs=[pl.BlockSpec((None, ts, D_MODEL), lambda g, t, j: (g, t, 0), pipeline_mode=pl.Buffered(1)),
                  _mod_spec(grp, ts, 0), _mod_spec(grp, ts, 1),
                  _const_spec((1, D_MODEL)),
                  pl.BlockSpec((D_MODEL, tn), lambda g, t, j: (0, j)),
                  pl.BlockSpec((D_MODEL, tn), lambda g, t, j: (0, nn + j)),
                  pl.BlockSpec((1, tn), lambda g, t, j: (0, j)),
                  pl.BlockSpec((1, tn), lambda g, t, j: (0, nn + j))],
        out_specs=pl.BlockSpec((None, ts, tn), lambda g, t, j: (g, t, j)),
        scratch_shapes=[pltpu.VMEM((ts, D_MODEL), BF)],
        compiler_params=_params("arbitrary", "arbitrary", "arbitrary"),
        name="conv_pw1_glu",
    )(x, mod, mod, norm_g, w_pw1, w_pw1, b_pw1, b_pw1)


def _ln_swish(y, g, b):
    mu = jnp.mean(y, axis=-1, keepdims=True)
    yc = y - mu
    var = jnp.mean(yc * yc, axis=-1, keepdims=True)
    return _silu(yc * lax.rsqrt(var + EPS) * g + b).astype(BF)


def _dwconv_kernel(u_ref, uh_ref, w_ref, b_ref, lg_ref, lb_ref, y_ref, ext_sc, sh_sc, acc_sc, *, ts):
    t = pl.program_id(1)
    ext_sc[0:HALO_CONV, :] = jnp.where(t > 0, uh_ref[...], 0.0)
    ext_sc[HALO_CONV:, :] = u_ref[...]
    rows = ts + HALO_CONV
    rc = 64
    lead = HALO_CONV - (CONV_W - 1)

    for c in range(D_MODEL // LANE):
        cols = slice(c * LANE, (c + 1) * LANE)
        x = ext_sc[:, cols]
        for r in range(1, SUBLANE):
            sh_sc[r] = pltpu.roll(x, rows - r, 0)

        def body(i, carry):
            r0 = pl.multiple_of(i * rc, rc)
            acc = jnp.zeros((rc, LANE), F32) + b_ref[:, cols]
            for k in range(CONV_W):
                a, r = divmod(lead + k, SUBLANE)
                if r == 0:
                    win = ext_sc[pl.ds(r0 + SUBLANE * a, rc), cols]
                else:
                    win = sh_sc[r, pl.ds(r0 + SUBLANE * a, rc), :]
                acc = acc + w_ref[k:k + 1, cols] * win
            acc_sc[pl.ds(r0, rc), cols] = acc
            return carry

        lax.fori_loop(0, ts // rc, body, 0)
    y_ref[...] = _ln_swish(acc_sc[...], lg_ref[...], lb_ref[...])


def _dwconv_prompt(ts, u, w_dw, b_dw, ln_g, ln_b):
    G, S, _ = u.shape
    hb = ts // HALO_CONV
    return pl.pallas_call(
        functools.partial(_dwconv_kernel, ts=ts),
        out_shape=jax.ShapeDtypeStruct((G, S, D_MODEL), BF),
        grid=(G, S // ts),
        in_specs=[_row_spec(ts, D_MODEL),
                  pl.BlockSpec((None, HALO_CONV, D_MODEL), lambda g, t: (g, jnp.maximum(t * hb - 1, 0), 0)),
                  _const_spec((CONV_W, D_MODEL)), _const_spec((1, D_MODEL)),
                  _const_spec((1, D_MODEL)), _const_spec((1, D_MODEL))],
        out_specs=_row_spec(ts, D_MODEL),
        scratch_shapes=[pltpu.VMEM((ts + HALO_CONV, D_MODEL), F32),
                        pltpu.VMEM((SUBLANE, ts + HALO_CONV, LANE), F32),
                        pltpu.VMEM((ts, D_MODEL), F32)],
        compiler_params=_params("arbitrary", "arbitrary"),
        name="dwconv_ln_swish",
    )(u, u, w_dw, b_dw, ln_g, ln_b)


def _dwconv_s_kernel(st_ref, u_ref, w_ref, b_ref, lg_ref, lb_ref, y_ref, ns_ref):
    nprev = CONV_W - 1
    u = u_ref[...]
    acc = b_ref[...] + w_ref[nprev:nprev + 1, :] * u
    for k in range(nprev):
        acc = acc + w_ref[k:k + 1, :] * st_ref[k]
    for k in range(nprev - 1):
        ns_ref[k] = st_ref[k + 1]
    ns_ref[nprev - 1] = u
    y_ref[...] = _ln_swish(acc, lg_ref[...], lb_ref[...])


def _dwconv_sample(rs, state_t, u, w_dw, b_dw, ln_g, ln_b):
    nprev, Bd, _ = state_t.shape
    st_spec = pl.BlockSpec((nprev, rs, D_MODEL), lambda i: (0, i, 0))
    return pl.pallas_call(
        _dwconv_s_kernel,
        out_shape=[jax.ShapeDtypeStruct((Bd, D_MODEL), BF), jax.ShapeDtypeStruct(state_t.shape, F32)],
        grid=(Bd // rs,),
        in_specs=[st_spec, pl.BlockSpec((rs, D_MODEL), lambda i: (i, 0)),
                  _const_spec((CONV_W, D_MODEL)), _const_spec((1, D_MODEL)),
                  _const_spec((1, D_MODEL)), _const_spec((1, D_MODEL))],
        out_specs=[pl.BlockSpec((rs, D_MODEL), lambda i: (i, 0)), st_spec],
        compiler_params=_params("arbitrary"),
        name="dwconv_ln_swish_sample",
    )(state_t, u, w_dw, b_dw, ln_g, ln_b)


def _rope_tables(pos):
    half = QK_ROPE // 2
    inv = ROPE_BASE ** (-jnp.arange(half, dtype=F32) / half)
    ang = pos.astype(F32)[:, None] * inv[None, :]
    cos, sin = jnp.cos(ang), jnp.sin(ang)
    z = jnp.zeros((pos.shape[0], LANE - QK_ROPE), F32)
    return jnp.concatenate([cos, cos, z], axis=1), jnp.concatenate([-sin, sin, z], axis=1)


def _swap_halves(w):
    half = QK_ROPE // 2
    return jnp.concatenate([w[..., half:], w[..., :half]], axis=-1)


def _mla_weights(wq_a, wq_b, wkv_a):
    z = jnp.zeros((D_MODEL, LANE - QK_ROPE), F32)
    pe = wkv_a[:, KV_LORA:]
    w_a = jnp.concatenate([wq_a, wkv_a[:, :KV_LORA], pe, z, _swap_halves(pe), z], axis=1).astype(BF)
    wqb = wq_b.reshape(Q_LORA, N_HEADS, QK_NOPE + QK_ROPE)
    nope, qpe = wqb[..., :QK_NOPE], wqb[..., QK_NOPE:]
    zq = jnp.zeros((Q_LORA, N_HEADS, LANE - QK_ROPE), F32)
    main = jnp.concatenate([nope, qpe, zq], axis=-1).reshape(Q_LORA, N_HEADS * HEAD_W)
    swapped = jnp.concatenate([_swap_halves(qpe), zq], axis=-1).reshape(Q_LORA, N_HEADS * LANE)
    return w_a, jnp.concatenate([main, swapped], axis=1).astype(BF)


def kernel(x_prompt, x_sample, cache_ckv, cache_kpe, state_conv, state_ffn, page_table, c_prompt, c_sample,
           w_mod, b_mod, norm_mix_g, norm_ffn_g, wq_a, q_norm_g, wq_b, wkv_a, kv_norm_g, w_uk, w_uv, w_o,
           conv_w_pw1, conv_b_pw1, conv_w_dw, conv_b_dw, conv_ln_g, conv_ln_b, conv_w_pw2, conv_b_pw2,
           ffn_w_gate, ffn_w_up, ffn_w_conv, ffn_b_conv, ffn_w_down, final_norm_g):
    D = D_MODEL
    B, S, _ = x_prompt.shape
    Bd = x_sample.shape[0]
    past_len = page_table.shape[1] * cache_ckv.shape[2]
    grp_p = Grp(B, S, False)
    grp_s = Grp(1, Bd, True)

    pad = (-(Bd + B)) % 8
    c_all = jnp.concatenate([c_sample, c_prompt, jnp.zeros((pad, D), F32)], axis=0)
    mod = _modulation(c_all, w_mod, b_mod)
    mod_s = [mod[i, :Bd][None] for i in range(2)]
    mod_p = [mod[i, Bd:Bd + B][:, None] for i in range(2)]

    row = lambda v: v.reshape(1, -1)
    xp = x_prompt
    xs = x_sample.reshape(1, Bd, D)

    w_a, wqb = _mla_weights(wq_a[0], wq_b[0], wkv_a[0])
    w_uk_flat = w_uk[0].reshape(KV_LORA, N_HEADS * QK_NOPE).astype(BF)
    w_uv_flat = w_uv[0].reshape(KV_LORA, N_HEADS * V_DIM).astype(BF)
    w_ukt = jnp.transpose(w_uk[0], (1, 2, 0)).astype(BF)
    w_o_b = w_o[0].astype(BF)
    cos_p, sin_p = _rope_tables(jnp.arange(S, dtype=jnp.int32))
    cos_s, sin_s = _rope_tables(jnp.full((Bd,), past_len, dtype=jnp.int32))
    ng, qg, kvg = row(norm_mix_g[0]), row(q_norm_g[0]), row(kv_norm_g[0])

    q_p, ckv_p, kpe_p, k_p, v_p = _mla_project(grp_p, 256, xp, mod_p[0], cos_p, sin_p, ng, w_a, qg, wqb, kvg,
                                               w_uk_flat, w_uv_flat)
    o_p = _prompt_attention(q_p, k_p, v_p, 256)
    xp = _out_residual(grp_p, 512, o_p, w_o_b, None, xp, mod_p[0], 2, "attn_out_residual")

    q_s, ckv_s, kpe_s = _mla_project(grp_s, Bd, xs, mod_s[0], cos_s, sin_s, ng, w_a, qg, wqb, kvg)
    qlat, qpe = _absorb_q(q_s[0], w_ukt)
    cache_kpet = jnp.swapaxes(cache_kpe, 2, 3)
    olat = _paged_decode(page_table, qlat.reshape(Bd, N_HEADS, KV_LORA), qpe.reshape(Bd, N_HEADS, LANE),
                         ckv_s.reshape(Bd, 1, KV_LORA), kpe_s.reshape(Bd, 1, QK_ROPE), cache_ckv, cache_kpet, 0)
    o_s = _unabsorb_o(olat.reshape(Bd, N_HEADS * KV_LORA), w_uv_flat)
    xs = _out_residual(grp_s, Bd, o_s[None], w_o_b, None, xs, mod_s[0], 2, "attn_out_residual_sample")

    fg = row(final_norm_g)
    bc = ffn_b_conv[:, None, :]

    def ffn(i, xp, xs, final_norm):
        ngf = row(norm_ffn_g[i])
        xs, st_s, wg, wu, wd = _conv_ffn(grp_s, Bd, 512, i, xs, mod_s[i], ngf, ffn_w_gate, ffn_w_up, ffn_w_conv, bc,
                                         ffn_w_down, fg, final_norm, state=state_ffn[i].reshape(Bd, 2 * D_FF))
        xp, st_p = _conv_ffn(grp_p, 1024, 512, i, xp, mod_p[i], ngf, wg, wu, ffn_w_conv, bc, wd, fg, final_norm)
        return xp, xs, st_p, st_s

    xp, xs, ffn_p0, ffn_s0 = ffn(0, xp, xs, False)

    w_pw1 = conv_w_pw1[0].astype(BF)
    b_pw1 = row(conv_b_pw1[0])
    w_pw2 = conv_w_pw2[0].astype(BF)
    b_pw2 = row(conv_b_pw2[0])
    ng1 = row(norm_mix_g[1])
    dw = (conv_w_dw[0], row(conv_b_dw[0]), row(conv_ln_g[0]), row(conv_ln_b[0]))

    u_p = _pw1_glu(grp_p, 1024, 1024, xp, mod_p[1], ng1, w_pw1, b_pw1)
    y_p = _dwconv_prompt(256, u_p, *dw)
    xp = _out_residual(grp_p, 512, y_p, w_pw2, b_pw2, xp, mod_p[1], 2, "conv_out_residual")
    conv_p = u_p[:, S - (CONV_W - 1):]

    u_s = _pw1_glu(grp_s, Bd, 512, xs, mod_s[1], ng1, w_pw1, b_pw1)
    y_s, conv_s = _dwconv_sample(16, jnp.swapaxes(state_conv[0], 0, 1), u_s[0], *dw)
    xs = _out_residual(grp_s, Bd, y_s[None], w_pw2, b_pw2, xs, mod_s[1], 2, "conv_out_residual_sample")

    yp, ys, ffn_p1, ffn_s1 = ffn(1, xp, xs, True)

    return (yp, ys.reshape(Bd, 1, D),
            ckv_p[None], kpe_p[None],
            ckv_s.reshape(1, Bd, 1, KV_LORA), kpe_s.reshape(1, Bd, 1, QK_ROPE),
            conv_p[None], jnp.swapaxes(conv_s, 0, 1)[None],
            jnp.stack([ffn_p0, ffn_p1]), jnp.stack([ffn_s0, ffn_s1]))
```

```python
import functools
import math
from typing import NamedTuple

import jax
import jax.numpy as jnp
from jax import lax
from jax.experimental import pallas as pl
from jax.experimental.pallas import tpu as pltpu

D_MODEL = 2048
N_HEADS = 16
Q_LORA = 512
KV_LORA = 512
QK_NOPE = 128
QK_ROPE = 64
V_DIM = 128
ATTN_SCALE = 1.0 / math.sqrt(QK_NOPE + QK_ROPE)
Q_SCALE = ATTN_SCALE * math.log2(math.e)
ROPE_BASE = 10000.0
CONV_W = 31
D_FF = 5632
EPS = 1e-6

HEAD_W = 256
LANE = 128
SUBLANE = 8
HALO_FFN = 8
HALO_CONV = 32
V7X_VMEM_LIMIT = 56 * 1024 * 1024

BF = jnp.bfloat16
F32 = jnp.float32


class Grp(NamedTuple):
    G: int
    S: int
    per_row: bool


def _params(*sem):
    return pltpu.CompilerParams(dimension_semantics=sem, vmem_limit_bytes=V7X_VMEM_LIMIT)


def _dot(a, b):
    return jnp.dot(a, b, preferred_element_type=F32)


def _dot_nt(a, b):
    return lax.dot_general(a, b, (((1,), (1,)), ((), ())), preferred_element_type=F32)


def _rms(x, g):
    return x * lax.rsqrt(jnp.mean(x * x, axis=-1, keepdims=True) + EPS) * g


def _silu(x):
    return x * jax.nn.sigmoid(x)


def _modulated_norm(x, g, sc, sh):
    return (_rms(x, g) * (1.0 + sc) + sh).astype(BF)


def _mod_spec(grp, ts, k):
    if grp.per_row:
        return pl.BlockSpec((None, ts, D_MODEL), lambda g, t, *_: (g, t, k))
    return pl.BlockSpec((None, 1, D_MODEL), lambda g, t, *_: (g, 0, k))


def _row_spec(ts, width):
    return pl.BlockSpec((None, ts, width), lambda g, t, *_: (g, t, 0))


def _const_spec(shape):
    nd = len(shape)
    return pl.BlockSpec(shape, lambda *_: (0,) * nd)


def _mod_kernel(c_ref, w_ref, b_ref, o_ref):
    c = c_ref[...]
    o_ref[...] = _dot(_silu(c).astype(BF), w_ref[...].astype(BF)) + b_ref[...]


def _modulation(c_all, w_mod, b_mod):
    L, D, N = w_mod.shape
    R = c_all.shape[0]
    tn = 1024
    return pl.pallas_call(
        _mod_kernel,
        out_shape=jax.ShapeDtypeStruct((L, R, N), F32),
        grid=(L, N // tn),
        in_specs=[pl.BlockSpec((R, D), lambda l, j: (0, 0)),
                  pl.BlockSpec((None, D, tn), lambda l, j: (l, 0, j)),
                  pl.BlockSpec((None, 1, tn), lambda l, j: (l, 0, j))],
        out_specs=pl.BlockSpec((None, R, tn), lambda l, j: (l, 0, j)),
        compiler_params=_params("arbitrary", "arbitrary"),
        name="modulation",
    )(c_all, w_mod, b_mod.reshape(L, 1, N))


def _proj_kernel(*refs, expand_kv):
    if expand_kv:
        (x_ref, sh_ref, sc_ref, cos_ref, sin_ref, ng_ref, wa_ref, qg_ref, wqb_ref, kvg_ref,
         wuk_ref, wuv_ref, q_ref, ckv_ref, kpe_ref, k_ref, v_ref) = refs
    else:
        (x_ref, sh_ref, sc_ref, cos_ref, sin_ref, ng_ref, wa_ref, qg_ref, wqb_ref, kvg_ref,
         q_ref, ckv_ref, kpe_ref) = refs
    h = _modulated_norm(x_ref[...], ng_ref[...], sc_ref[...], sh_ref[...])
    qkv = _dot(h, wa_ref[...])
    qn = _rms(qkv[:, :Q_LORA], qg_ref[...]).astype(BF)
    ckv = _rms(qkv[:, Q_LORA:Q_LORA + KV_LORA], kvg_ref[...])
    cos = cos_ref[...]
    sin = sin_ref[...]
    o = Q_LORA + KV_LORA
    kpe = qkv[:, o:o + LANE] * cos + qkv[:, o + LANE:o + 2 * LANE] * sin
    ckv_ref[...] = ckv
    kpe_ref[...] = kpe[:, :QK_ROPE]
    qm = _dot(qn, wqb_ref[...])
    sw0 = N_HEADS * HEAD_W
    for hd in range(N_HEADS):
        a = hd * HEAD_W
        q_ref[:, a:a + LANE] = (qm[:, a:a + LANE] * Q_SCALE).astype(BF)
        pe = qm[:, a + LANE:a + 2 * LANE] * cos + qm[:, sw0 + hd * LANE:sw0 + (hd + 1) * LANE] * sin
        q_ref[:, a + LANE:a + 2 * LANE] = (pe * Q_SCALE).astype(BF)
    if expand_kv:
        cb = ckv.astype(BF)
        kn = _dot(cb, wuk_ref[...])
        kpe_b = kpe.astype(BF)
        for hd in range(N_HEADS):
            a = hd * HEAD_W
            k_ref[:, a:a + LANE] = kn[:, hd * LANE:(hd + 1) * LANE].astype(BF)
            k_ref[:, a + LANE:a + 2 * LANE] = kpe_b
        v_ref[...] = _dot(cb, wuv_ref[...]).astype(BF)


def _mla_project(grp, ts, x, mod, cos, sin, norm_g, w_a, q_norm_g, wqb, kv_norm_g, w_uk=None, w_uv=None):
    expand_kv = w_uk is not None
    G, S = grp.G, grp.S
    qw = N_HEADS * HEAD_W
    in_specs = [_row_spec(ts, D_MODEL), _mod_spec(grp, ts, 0), _mod_spec(grp, ts, 1),
                pl.BlockSpec((ts, LANE), lambda g, t: (t, 0)), pl.BlockSpec((ts, LANE), lambda g, t: (t, 0)),
                _const_spec((1, D_MODEL)), _const_spec(w_a.shape), _const_spec((1, Q_LORA)),
                _const_spec(wqb.shape), _const_spec((1, KV_LORA))]
    args = [x, mod, mod, cos, sin, norm_g, w_a, q_norm_g, wqb, kv_norm_g]
    out_shape = [jax.ShapeDtypeStruct((G, S, qw), BF), jax.ShapeDtypeStruct((G, S, KV_LORA), F32),
                 jax.ShapeDtypeStruct((G, S, QK_ROPE), F32)]
    out_specs = [_row_spec(ts, qw), _row_spec(ts, KV_LORA), _row_spec(ts, QK_ROPE)]
    if expand_kv:
        in_specs += [_const_spec(w_uk.shape), _const_spec(w_uv.shape)]
        args += [w_uk, w_uv]
        out_shape += [jax.ShapeDtypeStruct((G, S, qw), BF), jax.ShapeDtypeStruct((G, S, N_HEADS * V_DIM), BF)]
        out_specs += [_row_spec(ts, qw), _row_spec(ts, N_HEADS * V_DIM)]
    return pl.pallas_call(
        functools.partial(_proj_kernel, expand_kv=expand_kv),
        out_shape=out_shape, grid=(G, S // ts), in_specs=in_specs, out_specs=out_specs,
        compiler_params=_params("arbitrary", "arbitrary"),
        name="mla_project_kv" if expand_kv else "mla_project",
    )(*args)


def _attn_kernel(q_ref, k_ref, v_ref, o_ref, *, tq, seq):
    row = lax.broadcasted_iota(jnp.int32, (tq, tq), 0)
    col = lax.broadcasted_iota(jnp.int32, (tq, tq), 1)
    causal = col <= row
    for i in range(seq // tq):
        lo, hi = i * tq, (i + 1) * tq
        s = _dot_nt(q_ref[lo:hi, :], k_ref[0:hi, :])
        sd = jnp.where(causal, s[:, lo:], -jnp.inf)
        m = jnp.max(sd, axis=-1, keepdims=True)
        if i > 0:
            so = s[:, :lo]
            m = jnp.maximum(m, jnp.max(so, axis=-1, keepdims=True))
        pd = jnp.exp2(sd - m)
        l = jnp.sum(pd, axis=-1, keepdims=True)
        p = pd.astype(BF)
        if i > 0:
            po = jnp.exp2(so - m)
            l = l + jnp.sum(po, axis=-1, keepdims=True)
            p = jnp.concatenate([po.astype(BF), p], axis=1)
        o_ref[lo:hi, :] = (_dot(p, v_ref[0:hi, :]) / l).astype(o_ref.dtype)


def _prompt_attention(q, k, v, tq):
    B, S, _ = q.shape
    return pl.pallas_call(
        functools.partial(_attn_kernel, tq=tq, seq=S),
        out_shape=jax.ShapeDtypeStruct((B, S, N_HEADS * V_DIM), BF),
        grid=(B, N_HEADS),
        in_specs=[pl.BlockSpec((None, S, HEAD_W), lambda b, h: (b, 0, h)),
                  pl.BlockSpec((None, S, HEAD_W), lambda b, h: (b, 0, h)),
                  pl.BlockSpec((None, S, V_DIM), lambda b, h: (b, 0, h))],
        out_specs=pl.BlockSpec((None, S, V_DIM), lambda b, h: (b, 0, h)),
        compiler_params=_params("arbitrary", "arbitrary"),
        name="prompt_attention",
    )(q, k, v)


def _absorb_kernel(q_ref, wukt_ref, qlat_ref, qpe_ref):
    q = q_ref[...]
    qlat_ref[...] = _dot(q[:, :QK_NOPE], wukt_ref[...]).astype(BF)
    qpe_ref[...] = q[:, QK_NOPE:]


def _absorb_q(q, w_ukt):
    Bd = q.shape[0]
    return pl.pallas_call(
        _absorb_kernel,
        out_shape=[jax.ShapeDtypeStruct((Bd, N_HEADS * KV_LORA), BF),
                   jax.ShapeDtypeStruct((Bd, N_HEADS * LANE), BF)],
        grid=(N_HEADS,),
        in_specs=[pl.BlockSpec((Bd, HEAD_W), lambda h: (0, h)),
                  pl.BlockSpec((None, QK_NOPE, KV_LORA), lambda h: (h, 0, 0))],
        out_specs=[pl.BlockSpec((Bd, KV_LORA), lambda h: (0, h)),
                   pl.BlockSpec((Bd, LANE), lambda h: (0, h))],
        compiler_params=_params("arbitrary"),
        name="absorb_q",
    )(q, w_ukt)


DEC_SLOTS = 8
DEC_AHEAD = 4


def _decode_kernel(pt_ref, qlat_ref, qpe_ref, ckvn_ref, kpen_ref, cckv_hbm, ckpet_hbm, o_ref,
                   ckv_buf, kpe_buf, kcb, sem, *, layer, n_pages, ch):
    b = pl.program_id(0)
    nb = pl.num_programs(0)
    n_ch = n_pages // ch
    page = ckv_buf.shape[2]

    def chunk_copies(bb, c):
        slot = c % DEC_SLOTS
        cps = []
        for i in range(ch):
            pg = pt_ref[bb * n_pages + c * ch + i]
            cps.append(pltpu.make_async_copy(cckv_hbm.at[layer, pg], ckv_buf.at[slot, i], sem.at[0, slot]))
            cps.append(pltpu.make_async_copy(ckpet_hbm.at[layer, pg], kpe_buf.at[slot, i], sem.at[1, slot]))
        return cps

    def start(bb, c):
        for n, cp in enumerate(chunk_copies(bb, c)):
            cp.start(priority=(n // 2) % 2)

    def start_ahead(c):
        if c < n_ch:
            start(b, c)
        else:
            @pl.when(b + 1 < nb)
            def _():
                start(b + 1, c - n_ch)

    @pl.when(b == 0)
    def _():
        for c in range(DEC_AHEAD):
            start(b, c)

    qlat = qlat_ref[...]
    qpe = qpe_ref[...][:, :QK_ROPE]
    ckvn = ckvn_ref[...].astype(BF).astype(F32)
    kpen = kpen_ref[...].astype(BF).astype(F32)

    def scores(c):
        for cp in chunk_copies(b, c):
            cp.wait()
        slot = c % DEC_SLOTS
        kc = ckv_buf[slot].reshape(ch * page, KV_LORA).astype(BF)
        kcb[c % 2] = kc
        pe = [_dot(qpe, kpe_buf[slot, i].astype(BF)) for i in range(ch)]
        return _dot_nt(qlat, kc) + jnp.concatenate(pe, axis=1)

    m = (jnp.sum(qlat.astype(F32) * ckvn, axis=-1, keepdims=True)
         + jnp.sum(qpe.astype(F32) * kpen, axis=-1, keepdims=True))
    l = jnp.ones_like(m)
    acc = jnp.broadcast_to(ckvn, (N_HEADS, KV_LORA))

    s = scores(0)
    for c in range(n_ch):
        start_ahead(c + DEC_AHEAD)
        s_next = scores(c + 1) if c + 1 < n_ch else None
        m_new = jnp.maximum(m, jnp.max(s, axis=-1, keepdims=True))
        alpha = jnp.exp2(m - m_new)
        p = jnp.exp2(s - m_new)
        l = alpha * l + jnp.sum(p, axis=-1, keepdims=True)
        acc = alpha * acc + _dot(p.astype(BF), kcb[c % 2])
        m = m_new
        s = s_next
    o_ref[...] = (acc / l).astype(o_ref.dtype)


def _paged_decode(page_table, qlat, qpe, ckv_new, kpe_new, cache_ckv, cache_kpet, layer, ch=8):
    Bd, n_pages = page_table.shape
    page = cache_ckv.shape[2]
    n_ch = n_pages // ch
    assert n_pages % ch == 0 and n_ch % DEC_SLOTS == 0 and n_ch >= DEC_AHEAD
    grid_spec = pltpu.PrefetchScalarGridSpec(
        num_scalar_prefetch=1,
        grid=(Bd,),
        in_specs=[pl.BlockSpec((None, N_HEADS, KV_LORA), lambda b, pt: (b, 0, 0)),
                  pl.BlockSpec((None, N_HEADS, LANE), lambda b, pt: (b, 0, 0)),
                  pl.BlockSpec((None, 1, KV_LORA), lambda b, pt: (b, 0, 0)),
                  pl.BlockSpec((None, 1, QK_ROPE), lambda b, pt: (b, 0, 0)),
                  pl.BlockSpec(memory_space=pl.ANY),
                  pl.BlockSpec(memory_space=pl.ANY)],
        out_specs=pl.BlockSpec((None, N_HEADS, KV_LORA), lambda b, pt: (b, 0, 0)),
        scratch_shapes=[pltpu.VMEM((DEC_SLOTS, ch, page, KV_LORA), F32),
                        pltpu.VMEM((DEC_SLOTS, ch, QK_ROPE, page), F32),
                        pltpu.VMEM((2, ch * page, KV_LORA), BF),
                        pltpu.SemaphoreType.DMA((2, DEC_SLOTS))],
    )
    return pl.pallas_call(
        functools.partial(_decode_kernel, layer=layer, n_pages=n_pages, ch=ch),
        out_shape=jax.ShapeDtypeStruct((Bd, N_HEADS, KV_LORA), BF),
        grid_spec=grid_spec,
        compiler_params=_params("arbitrary"),
        name="paged_decode",
    )(page_table.reshape(-1), qlat, qpe, ckv_new, kpe_new, cache_ckv, cache_kpet)


def _unabsorb_kernel(olat_ref, wuv_ref, o_ref):
    o_ref[...] = _dot(olat_ref[...], wuv_ref[...]).astype(o_ref.dtype)


def _unabsorb_o(olat, w_uv):
    Bd = olat.shape[0]
    return pl.pallas_call(
        _unabsorb_kernel,
        out_shape=jax.ShapeDtypeStruct((Bd, N_HEADS * V_DIM), BF),
        grid=(N_HEADS,),
        in_specs=[pl.BlockSpec((Bd, KV_LORA), lambda h: (0, h)),
                  pl.BlockSpec((KV_LORA, V_DIM), lambda h: (0, h))],
        out_specs=pl.BlockSpec((Bd, V_DIM), lambda h: (0, h)),
        compiler_params=_params("arbitrary"),
        name="unabsorb_o",
    )(olat, w_uv)


def _out_kernel(*refs, has_bias):
    if has_bias:
        a_ref, w_ref, b_ref, x_ref, gt_ref, o_ref = refs
    else:
        a_ref, w_ref, x_ref, gt_ref, o_ref = refs
    y = _dot(a_ref[...], w_ref[...])
    if has_bias:
        y = y + b_ref[...]
    o_ref[...] = x_ref[...] + gt_ref[...] * y


def _out_residual(grp, ts, a, w, bias, x, mod, gate_k, name):
    G, S = grp.G, grp.S
    K = a.shape[-1]
    has_bias = bias is not None
    in_specs = [_row_spec(ts, K), _const_spec(w.shape)]
    args = [a, w]
    if has_bias:
        in_specs.append(_const_spec((1, D_MODEL)))
        args.append(bias)
    in_specs += [_row_spec(ts, D_MODEL), _mod_spec(grp, ts, gate_k)]
    args += [x, mod]
    return pl.pallas_call(
        functools.partial(_out_kernel, has_bias=has_bias),
        out_shape=jax.ShapeDtypeStruct((G, S, D_MODEL), F32),
        grid=(G, S // ts), in_specs=in_specs, out_specs=_row_spec(ts, D_MODEL),
        compiler_params=_params("arbitrary", "arbitrary"),
        name=name,
    )(*args)


def _ffn_kernel(*refs, sample, final_norm, ts):
    if sample:
        (x_ref, s0_ref, s1_ref, sh_ref, sc_ref, gt_ref, ng_ref, wg_ref, wu_ref, wc_ref, bc_ref, wd_ref, fg_ref,
         o_ref, ns0_ref, ns1_ref, wgb_ref, wub_ref, wdb_ref, h_sc) = refs
    else:
        (x_ref, xh_ref, sh_ref, sc_ref, gt_ref, ng_ref, wg_ref, wu_ref, wc_ref, bc_ref, wd_ref, fg_ref,
         o_ref, gs_ref, h_sc, hh_sc) = refs
    t = pl.program_id(1)
    j = pl.program_id(2)
    nj = pl.num_programs(2)

    def step(first, last):
        if first:
            h = _modulated_norm(x_ref[...], ng_ref[...], sc_ref[...], sh_ref[...])
            h_sc[...] = h
            if not sample:
                hh = _modulated_norm(xh_ref[...], ng_ref[...], sc_ref[...], sh_ref[...])
                hh_sc[...] = hh
        else:
            h = h_sc[...]
            if not sample:
                hh = hh_sc[...]
        if sample:
            wg = wg_ref[...].astype(BF)
            wu = wu_ref[...].astype(BF)
            wd = wd_ref[...].astype(BF)
            wgb_ref[...] = wg
            wub_ref[...] = wu
            wdb_ref[...] = wd
        else:
            wg, wu, wd = wg_ref[...], wu_ref[...], wd_ref[...]
        g = _dot(h, wg)
        u = _dot(h, wu)
        wc = wc_ref[...]
        if sample:
            s1 = s1_ref[...]
            gc = wc[0:1] * s0_ref[...] + wc[1:2] * s1 + wc[2:3] * g + bc_ref[...]
            ns0_ref[...] = s1
            ns1_ref[...] = g
        else:
            gh = jnp.where(t > 0, _dot(hh, wg), 0.0)
            gs_ref[...] = g[ts - 2:ts, :]
            row = lax.broadcasted_iota(jnp.int32, g.shape, 0)
            p1 = gh[HALO_FFN - 1:HALO_FFN]
            p2 = gh[HALO_FFN - 2:HALO_FFN - 1]
            g1 = jnp.where(row == 0, p1, pltpu.roll(g, 1, 0))
            g2 = jnp.where(row == 0, p2, jnp.where(row == 1, p1, pltpu.roll(g, 2, 0)))
            gc = wc[0:1] * g2 + wc[1:2] * g1 + wc[2:3] * g + bc_ref[...]
        a = (_silu(gc) * u).astype(BF)
        acc = _dot(a, wd)
        if not first:
            acc = o_ref[...] + acc
        if last:
            acc = x_ref[...] + gt_ref[...] * acc
            if final_norm:
                acc = _rms(acc, fg_ref[...])
        o_ref[...] = acc

    @pl.when(j == 0)
    def _():
        step(True, False)

    @pl.when(jnp.logical_and(j > 0, j < nj - 1))
    def _():
        step(False, False)

    @pl.when(j == nj - 1)
    def _():
        step(False, True)


def _conv_ffn(grp, ts, tf, layer, x, mod, norm_g, w_gate, w_up, w_conv, b_conv, w_down, final_g, final_norm,
              state=None):
    G, S = grp.G, grp.S
    sample = state is not None
    nf = D_FF // tf
    nt = S // ts
    assert nf >= 2
    args = [x]
    if sample:
        in_specs = [_row_spec(ts, D_MODEL),
                    pl.BlockSpec((ts, tf), lambda g, t, j: (t, j)),
                    pl.BlockSpec((ts, tf), lambda g, t, j: (t, nf + j))]
        args += [state, state]
        w_specs = [pl.BlockSpec((None, D_MODEL, tf), lambda g, t, j: (layer, 0, j)),
                   pl.BlockSpec((None, D_MODEL, tf), lambda g, t, j: (layer, 0, j)),
                   pl.BlockSpec((None, tf, D_MODEL), lambda g, t, j: (layer, j, 0))]
    else:
        hb = ts // HALO_FFN
        in_specs = [pl.BlockSpec((None, ts, D_MODEL), lambda g, t, j: (g, t, 0), pipeline_mode=pl.Buffered(1)),
                    pl.BlockSpec((None, HALO_FFN, D_MODEL), lambda g, t, j: (g, jnp.maximum(t * hb - 1, 0), 0))]
        args.append(x)
        w_specs = [pl.BlockSpec((D_MODEL, tf), lambda g, t, j: (0, j)),
                   pl.BlockSpec((D_MODEL, tf), lambda g, t, j: (0, j)),
                   pl.BlockSpec((tf, D_MODEL), lambda g, t, j: (j, 0))]
    in_specs += [_mod_spec(grp, ts, 3), _mod_spec(grp, ts, 4), _mod_spec(grp, ts, 5),
                 _const_spec((1, D_MODEL)), w_specs[0], w_specs[1],
                 pl.BlockSpec((None, 3, tf), lambda g, t, j: (layer, 0, j)),
                 pl.BlockSpec((None, 1, tf), lambda g, t, j: (layer, 0, j)),
                 w_specs[2], _const_spec((1, D_MODEL))]
    args += [mod, mod, mod, norm_g, w_gate, w_up, w_conv, b_conv, w_down, final_g]
    out_shape = [jax.ShapeDtypeStruct((G, S, D_MODEL), F32)]
    out_specs = [_row_spec(ts, D_MODEL)]
    scratch = [pltpu.VMEM((ts, D_MODEL), BF)]
    if sample:
        assert G == 1 and nt == 1
        out_shape += [jax.ShapeDtypeStruct((S, D_FF), F32)] * 2
        out_specs += [pl.BlockSpec((ts, tf), lambda g, t, j: (t, j))] * 2
        out_shape += [jax.ShapeDtypeStruct((D_MODEL, D_FF), BF)] * 2 + [jax.ShapeDtypeStruct((D_FF, D_MODEL), BF)]
        out_specs += [pl.BlockSpec((D_MODEL, tf), lambda g, t, j: (0, j)),
                      pl.BlockSpec((D_MODEL, tf), lambda g, t, j: (0, j)),
                      pl.BlockSpec((tf, D_MODEL), lambda g, t, j: (j, 0))]
    else:
        out_shape.append(jax.ShapeDtypeStruct((G, nt, 2, D_FF), F32))
        out_specs.append(pl.BlockSpec((None, None, 2, tf), lambda g, t, j: (g, t, 0, j)))
        scratch.append(pltpu.VMEM((HALO_FFN, D_MODEL), BF))
    outs = pl.pallas_call(
        functools.partial(_ffn_kernel, sample=sample, final_norm=final_norm, ts=ts),
        out_shape=out_shape, grid=(G, nt, nf), in_specs=in_specs, out_specs=out_specs,
        scratch_shapes=scratch,
        compiler_params=_params("arbitrary", "arbitrary", "arbitrary"),
        name="conv_ffn_sample" if sample else "conv_ffn",
    )(*args)
    if sample:
        return outs[0], jnp.stack([outs[1], outs[2]], axis=1), outs[3], outs[4], outs[5]
    return outs[0], outs[1][:, -1]


def _pw1_kernel(x_ref, sh_ref, sc_ref, ng_ref, wa_ref, wb_ref, ba_ref, bb_ref, u_ref, h_sc):
    def step(h):
        a = _dot(h, wa_ref[...]) + ba_ref[...]
        b = _dot(h, wb_ref[...]) + bb_ref[...]
        u_ref[...] = a * jax.nn.sigmoid(b)

    @pl.when(pl.program_id(2) == 0)
    def _():
        h = _modulated_norm(x_ref[...], ng_ref[...], sc_ref[...], sh_ref[...])
        h_sc[...] = h
        step(h)

    @pl.when(pl.program_id(2) > 0)
    def _():
        step(h_sc[...])


def _pw1_glu(grp, ts, tn, x, mod, norm_g, w_pw1, b_pw1):
    G, S = grp.G, grp.S
    nn = D_MODEL // tn
    return pl.pallas_call(
        _pw1_kernel,
        out_shape=jax.ShapeDtypeStruct((G, S, D_MODEL), F32),
        grid=(G, S // ts, nn),
        in_specs=[_row_spec(ts, D_MODEL), _mod_spec(grp, ts, 0), _mod_spec(grp, ts, 1),
                  _const_spec((1, D_MODEL)),
                  pl.BlockSpec((D_MODEL, tn), lambda g, t, j: (0, j)),
                  pl.BlockSpec((D_MODEL, tn), lambda g, t, j: (0, nn + j)),
                  pl.BlockSpec((1, tn), lambda g, t, j: (0, j)),
                  pl.BlockSpec((1, tn), lambda g, t, j: (0, nn + j))],
        out_specs=pl.BlockSpec((None, ts, tn), lambda g, t, j: (g, t, j)),
        scratch_shapes=[pltpu.VMEM((ts, D_MODEL), BF)],
        compiler_params=_params("arbitrary", "arbitrary", "arbitrary"),
        name="conv_pw1_glu",
    )(x, mod, mod, norm_g, w_pw1, w_pw1, b_pw1, b_pw1)


def _ln_swish(y, g, b):
    mu = jnp.mean(y, axis=-1, keepdims=True)
    yc = y - mu
    var = jnp.mean(yc * yc, axis=-1, keepdims=True)
    return _silu(yc * lax.rsqrt(var + EPS) * g + b).astype(BF)


def _dwconv_kernel(u_ref, uh_ref, w_ref, b_ref, lg_ref, lb_ref, y_ref, ext_sc, sh_sc, acc_sc, *, ts):
    t = pl.program_id(1)
    ext_sc[0:HALO_CONV, :] = jnp.where(t > 0, uh_ref[...], 0.0)
    ext_sc[HALO_CONV:, :] = u_ref[...]
    rows = ts + HALO_CONV
    rc = 64
    lead = HALO_CONV - (CONV_W - 1)

    for c in range(D_MODEL // LANE):
        cols = slice(c * LANE, (c + 1) * LANE)
        x = ext_sc[:, cols]
        for r in range(1, SUBLANE):
            sh_sc[r] = pltpu.roll(x, rows - r, 0)

        def body(i, carry):
            r0 = pl.multiple_of(i * rc, rc)
            acc = jnp.zeros((rc, LANE), F32) + b_ref[:, cols]
            for k in range(CONV_W):
                a, r = divmod(lead + k, SUBLANE)
                if r == 0:
                    win = ext_sc[pl.ds(r0 + SUBLANE * a, rc), cols]
                else:
                    win = sh_sc[r, pl.ds(r0 + SUBLANE * a, rc), :]
                acc = acc + w_ref[k:k + 1, cols] * win
            acc_sc[pl.ds(r0, rc), cols] = acc
            return carry

        lax.fori_loop(0, ts // rc, body, 0)
    y_ref[...] = _ln_swish(acc_sc[...], lg_ref[...], lb_ref[...])


def _dwconv_prompt(ts, u, w_dw, b_dw, ln_g, ln_b):
    G, S, _ = u.shape
    hb = ts // HALO_CONV
    return pl.pallas_call(
        functools.partial(_dwconv_kernel, ts=ts),
        out_shape=jax.ShapeDtypeStruct((G, S, D_MODEL), BF),
        grid=(G, S // ts),
        in_specs=[_row_spec(ts, D_MODEL),
                  pl.BlockSpec((None, HALO_CONV, D_MODEL), lambda g, t: (g, jnp.maximum(t * hb - 1, 0), 0)),
                  _const_spec((CONV_W, D_MODEL)), _const_spec((1, D_MODEL)),
                  _const_spec((1, D_MODEL)), _const_spec((1, D_MODEL))],
        out_specs=_row_spec(ts, D_MODEL),
        scratch_shapes=[pltpu.VMEM((ts + HALO_CONV, D_MODEL), F32),
                        pltpu.VMEM((SUBLANE, ts + HALO_CONV, LANE), F32),
                        pltpu.VMEM((ts, D_MODEL), F32)],
        compiler_params=_params("arbitrary", "arbitrary"),
        name="dwconv_ln_swish",
    )(u, u, w_dw, b_dw, ln_g, ln_b)


def _dwconv_out_kernel(u_ref, uh_ref, w_ref, b_ref, lg_ref, lb_ref, wp_ref, bp_ref, x_ref, gt_ref, o_ref,
                       ext_sc, sh_sc, acc_sc, y_sc, *, ts, nt):
    t = pl.program_id(1)
    rows = ts + HALO_CONV
    rc = 64
    lead = HALO_CONV - (CONV_W - 1)

    def pointwise():
        o_ref[...] = x_ref[...] + gt_ref[...] * (_dot(y_sc[...], wp_ref[...]) + bp_ref[...])

    def conv():
        ext_sc[0:HALO_CONV, :] = jnp.where(t > 0, uh_ref[...], 0.0)
        ext_sc[HALO_CONV:, :] = u_ref[...]
        for c in range(D_MODEL // LANE):
            cols = slice(c * LANE, (c + 1) * LANE)
            sh = sh_sc.at[c % 2]
            x = ext_sc[:, cols]
            for r in range(1, SUBLANE):
                sh[r] = pltpu.roll(x, rows - r, 0)
            for r0 in range(0, ts, rc):
                acc = jnp.zeros((rc, LANE), F32) + b_ref[:, cols]
                for k in range(CONV_W):
                    a, r = divmod(lead + k, SUBLANE)
                    lo = r0 + SUBLANE * a
                    win = ext_sc[lo:lo + rc, cols] if r == 0 else sh[r, lo:lo + rc, :]
                    acc = acc + w_ref[k:k + 1, cols] * win
                acc_sc[r0:r0 + rc, cols] = acc
        y_sc[...] = _ln_swish(acc_sc[...], lg_ref[...], lb_ref[...])

    @pl.when(t == 0)
    def _():
        conv()

    @pl.when(jnp.logical_and(t > 0, t < nt))
    def _():
        pointwise()
        conv()

    @pl.when(t == nt)
    def _():
        pointwise()


def _dwconv_out_prompt(grp, ts, u, w_dw, b_dw, ln_g, ln_b, w_pw2, b_pw2, x, mod, gate_k):
    G, S = grp.G, grp.S
    nt = S // ts
    hb = ts // HALO_CONV
    conv_t = lambda t: jnp.minimum(t, nt - 1)
    out_t = lambda t: jnp.maximum(t - 1, 0)
    return pl.pallas_call(
        functools.partial(_dwconv_out_kernel, ts=ts, nt=nt),
        out_shape=jax.ShapeDtypeStruct((G, S, D_MODEL), F32),
        grid=(G, nt + 1),
        in_specs=[pl.BlockSpec((None, ts, D_MODEL), lambda g, t: (g, conv_t(t), 0)),
                  pl.BlockSpec((None, HALO_CONV, D_MODEL),
                               lambda g, t: (g, jnp.maximum(conv_t(t) * hb - 1, 0), 0)),
                  _const_spec((CONV_W, D_MODEL)), _const_spec((1, D_MODEL)),
                  _const_spec((1, D_MODEL)), _const_spec((1, D_MODEL)),
                  _const_spec(w_pw2.shape), _const_spec((1, D_MODEL)),
                  pl.BlockSpec((None, ts, D_MODEL), lambda g, t: (g, out_t(t), 0)),
                  pl.BlockSpec((None, 1, D_MODEL), lambda g, t: (g, 0, gate_k))],
        out_specs=pl.BlockSpec((None, ts, D_MODEL), lambda g, t: (g, out_t(t), 0)),
        scratch_shapes=[pltpu.VMEM((ts + HALO_CONV, D_MODEL), F32),
                        pltpu.VMEM((2, SUBLANE, ts + HALO_CONV, LANE), F32),
                        pltpu.VMEM((ts, D_MODEL), F32),
                        pltpu.VMEM((ts, D_MODEL), BF)],
        compiler_params=_params("arbitrary", "arbitrary"),
        name="dwconv_out_residual",
    )(u, u, w_dw, b_dw, ln_g, ln_b, w_pw2, b_pw2, x, mod)


def _dwconv_s_kernel(st_ref, u_ref, w_ref, b_ref, lg_ref, lb_ref, y_ref, ns_ref):
    nprev = CONV_W - 1
    u = u_ref[...]
    acc = b_ref[...] + w_ref[nprev:nprev + 1, :] * u
    for k in range(nprev):
        acc = acc + w_ref[k:k + 1, :] * st_ref[k]
    for k in range(nprev - 1):
        ns_ref[k] = st_ref[k + 1]
    ns_ref[nprev - 1] = u
    y_ref[...] = _ln_swish(acc, lg_ref[...], lb_ref[...])


def _dwconv_sample(rs, state_t, u, w_dw, b_dw, ln_g, ln_b):
    nprev, Bd, _ = state_t.shape
    st_spec = pl.BlockSpec((nprev, rs, D_MODEL), lambda i: (0, i, 0))
    return pl.pallas_call(
        _dwconv_s_kernel,
        out_shape=[jax.ShapeDtypeStruct((Bd, D_MODEL), BF), jax.ShapeDtypeStruct(state_t.shape, F32)],
        grid=(Bd // rs,),
        in_specs=[st_spec, pl.BlockSpec((rs, D_MODEL), lambda i: (i, 0)),
                  _const_spec((CONV_W, D_MODEL)), _const_spec((1, D_MODEL)),
                  _const_spec((1, D_MODEL)), _const_spec((1, D_MODEL))],
        out_specs=[pl.BlockSpec((rs, D_MODEL), lambda i: (i, 0)), st_spec],
        compiler_params=_params("arbitrary"),
        name="dwconv_ln_swish_sample",
    )(state_t, u, w_dw, b_dw, ln_g, ln_b)


def _rope_tables(pos):
    half = QK_ROPE // 2
    inv = ROPE_BASE ** (-jnp.arange(half, dtype=F32) / half)
    ang = pos.astype(F32)[:, None] * inv[None, :]
    cos, sin = jnp.cos(ang), jnp.sin(ang)
    z = jnp.zeros((pos.shape[0], LANE - QK_ROPE), F32)
    return jnp.concatenate([cos, cos, z], axis=1), jnp.concatenate([-sin, sin, z], axis=1)


def _swap_halves(w):
    half = QK_ROPE // 2
    return jnp.concatenate([w[..., half:], w[..., :half]], axis=-1)


def _mla_weights(wq_a, wq_b, wkv_a):
    z = jnp.zeros((D_MODEL, LANE - QK_ROPE), F32)
    pe = wkv_a[:, KV_LORA:]
    w_a = jnp.concatenate([wq_a, wkv_a[:, :KV_LORA], pe, z, _swap_halves(pe), z], axis=1).astype(BF)
    wqb = wq_b.reshape(Q_LORA, N_HEADS, QK_NOPE + QK_ROPE)
    nope, qpe = wqb[..., :QK_NOPE], wqb[..., QK_NOPE:]
    zq = jnp.zeros((Q_LORA, N_HEADS, LANE - QK_ROPE), F32)
    main = jnp.concatenate([nope, qpe, zq], axis=-1).reshape(Q_LORA, N_HEADS * HEAD_W)
    swapped = jnp.concatenate([_swap_halves(qpe), zq], axis=-1).reshape(Q_LORA, N_HEADS * LANE)
    return w_a, jnp.concatenate([main, swapped], axis=1).astype(BF)


def kernel(x_prompt, x_sample, cache_ckv, cache_kpe, state_conv, state_ffn, page_table, c_prompt, c_sample,
           w_mod, b_mod, norm_mix_g, norm_ffn_g, wq_a, q_norm_g, wq_b, wkv_a, kv_norm_g, w_uk, w_uv, w_o,
           conv_w_pw1, conv_b_pw1, conv_w_dw, conv_b_dw, conv_ln_g, conv_ln_b, conv_w_pw2, conv_b_pw2,
           ffn_w_gate, ffn_w_up, ffn_w_conv, ffn_b_conv, ffn_w_down, final_norm_g):
    D = D_MODEL
    B, S, _ = x_prompt.shape
    Bd = x_sample.shape[0]
    past_len = page_table.shape[1] * cache_ckv.shape[2]
    grp_p = Grp(B, S, False)
    grp_s = Grp(1, Bd, True)

    pad = (-(Bd + B)) % 8
    c_all = jnp.concatenate([c_sample, c_prompt, jnp.zeros((pad, D), F32)], axis=0)
    mod = _modulation(c_all, w_mod, b_mod)
    mod_s = [mod[i, :Bd][None] for i in range(2)]
    mod_p = [mod[i, Bd:Bd + B][:, None] for i in range(2)]

    row = lambda v: v.reshape(1, -1)
    xp = x_prompt
    xs = x_sample.reshape(1, Bd, D)

    w_a, wqb = _mla_weights(wq_a[0], wq_b[0], wkv_a[0])
    w_uk_flat = w_uk[0].reshape(KV_LORA, N_HEADS * QK_NOPE).astype(BF)
    w_uv_flat = w_uv[0].reshape(KV_LORA, N_HEADS * V_DIM).astype(BF)
    w_ukt = jnp.transpose(w_uk[0], (1, 2, 0)).astype(BF)
    w_o_b = w_o[0].astype(BF)
    cos_p, sin_p = _rope_tables(jnp.arange(S, dtype=jnp.int32))
    cos_s, sin_s = _rope_tables(jnp.full((Bd,), past_len, dtype=jnp.int32))
    ng, qg, kvg = row(norm_mix_g[0]), row(q_norm_g[0]), row(kv_norm_g[0])

    q_p, ckv_p, kpe_p, k_p, v_p = _mla_project(grp_p, 256, xp, mod_p[0], cos_p, sin_p, ng, w_a, qg, wqb, kvg,
                                               w_uk_flat, w_uv_flat)
    o_p = _prompt_attention(q_p, k_p, v_p, 256)
    xp = _out_residual(grp_p, 512, o_p, w_o_b, None, xp, mod_p[0], 2, "attn_out_residual")

    q_s, ckv_s, kpe_s = _mla_project(grp_s, Bd, xs, mod_s[0], cos_s, sin_s, ng, w_a, qg, wqb, kvg)
    qlat, qpe = _absorb_q(q_s[0], w_ukt)
    cache_kpet = jnp.swapaxes(cache_kpe, 2, 3)
    olat = _paged_decode(page_table, qlat.reshape(Bd, N_HEADS, KV_LORA), qpe.reshape(Bd, N_HEADS, LANE),
                         ckv_s.reshape(Bd, 1, KV_LORA), kpe_s.reshape(Bd, 1, QK_ROPE), cache_ckv, cache_kpet, 0)
    o_s = _unabsorb_o(olat.reshape(Bd, N_HEADS * KV_LORA), w_uv_flat)
    xs = _out_residual(grp_s, Bd, o_s[None], w_o_b, None, xs, mod_s[0], 2, "attn_out_residual_sample")

    fg = row(final_norm_g)
    bc = ffn_b_conv[:, None, :]

    def ffn(i, xp, xs, final_norm):
        ngf = row(norm_ffn_g[i])
        xs, st_s, wg, wu, wd = _conv_ffn(grp_s, Bd, 512, i, xs, mod_s[i], ngf, ffn_w_gate, ffn_w_up, ffn_w_conv, bc,
                                         ffn_w_down, fg, final_norm, state=state_ffn[i].reshape(Bd, 2 * D_FF))
        xp, st_p = _conv_ffn(grp_p, 1024, 512, i, xp, mod_p[i], ngf, wg, wu, ffn_w_conv, bc, wd, fg, final_norm)
        return xp, xs, st_p, st_s

    xp, xs, ffn_p0, ffn_s0 = ffn(0, xp, xs, False)

    w_pw1 = conv_w_pw1[0].astype(BF)
    b_pw1 = row(conv_b_pw1[0])
    w_pw2 = conv_w_pw2[0].astype(BF)
    b_pw2 = row(conv_b_pw2[0])
    ng1 = row(norm_mix_g[1])
    dw = (conv_w_dw[0], row(conv_b_dw[0]), row(conv_ln_g[0]), row(conv_ln_b[0]))

    u_p = _pw1_glu(grp_p, 1024, 512, xp, mod_p[1], ng1, w_pw1, b_pw1)
    xp = _dwconv_out_prompt(grp_p, 256, u_p, *dw, w_pw2, b_pw2, xp, mod_p[1], 2)
    conv_p = u_p[:, S - (CONV_W - 1):]

    u_s = _pw1_glu(grp_s, Bd, 512, xs, mod_s[1], ng1, w_pw1, b_pw1)
    y_s, conv_s = _dwconv_sample(16, jnp.swapaxes(state_conv[0], 0, 1), u_s[0], *dw)
    xs = _out_residual(grp_s, Bd, y_s[None], w_pw2, b_pw2, xs, mod_s[1], 2, "conv_out_residual_sample")

    yp, ys, ffn_p1, ffn_s1 = ffn(1, xp, xs, True)

    return (yp, ys.reshape(Bd, 1, D),
            ckv_p[None], kpe_p[None],
            ckv_s.reshape(1, Bd, 1, KV_LORA), kpe_s.reshape(1, Bd, 1, QK_ROPE),
            conv_p[None], jnp.swapaxes(conv_s, 0, 1)[None],
            jnp.stack([ffn_p0, ffn_p1]), jnp.stack([ffn_s0, ffn_s1]))
```
